```python
import jax, jax.numpy as jnp
from jax import lax
import numpy as np

D_MODEL = 1024
BATCH = 1
SEQ = 16384
DEPTH = 2
DEC_BATCH = 32
DEC_SEQ = 1
PAST_LEN = 16384
PAGE_SIZE = 128

N_MIXERS = 2
N_A_LAYERS = (DEPTH + 1) // 2
N_B_LAYERS = DEPTH // 2
DSW_GROUPS = ((128, 1), (512, 4), (2048, 16))
N_GROUPS = len(DSW_GROUPS)
A_HEADS = 16
A_HEAD_DIM = D_MODEL // A_HEADS
A_WIDTH = A_HEADS * A_HEAD_DIM
ROPE_THETA = 10000.0
Q_BLOCK = 128
B_HEADS = 4
B_QK_DIM = D_MODEL // (2 * B_HEADS)
B_V_DIM = D_MODEL // B_HEADS
B_CONV = 4
B_CHUNK = 128
N_EXPERTS = 32
TOP_K = 4
D_EXPERT = D_MODEL
SWIGLU_LIMIT = 7.0
SWIGLU_ALPHA = 1.702
DN_ALPHA = (2 * DEPTH) ** 0.25
DN_BETA = (8 * DEPTH) ** -0.25
LN_EPS = 1e-5

kernel_name = 'dilated_swa_mlstm_moe_deepnorm_step'


def layer_norm(x, g, b):
    xf = x.astype(jnp.float32)
    mu = xf.mean(-1, keepdims=True)
    var = jnp.square(xf - mu).mean(-1, keepdims=True)
    return ((xf - mu) * lax.rsqrt(var + LN_EPS) * g + b).astype(x.dtype)


def rope(x, pos):
    half = x.shape[-1] // 2
    inv = ROPE_THETA ** (-jnp.arange(half, dtype=jnp.float32) / half)
    ang = pos.astype(jnp.float32)[:, None] * inv[None, :]
    shape = (1, x.shape[1]) + (1,) * (x.ndim - 3) + (half,)
    cos = jnp.cos(ang).reshape(shape)
    sin = jnp.sin(ang).reshape(shape)
    xf = x.astype(jnp.float32)
    x1, x2 = xf[..., :half], xf[..., half:]
    return jnp.concatenate([x1 * cos - x2 * sin, x2 * cos + x1 * sin], axis=-1).astype(x.dtype)


def dilated_mixture_attn(q, kvs, offsets):
    B, T = q.shape[:2]
    scale = A_HEAD_DIM ** -0.5

    def block(q_blk, start):
        Tb = q_blk.shape[1]
        outs, lses = [], []
        for g, (window, dil) in enumerate(DSW_GROUPS):
            n_keys = window // dil + 1
            idx = offsets[g] + start + jnp.arange(Tb)[:, None] - dil * jnp.arange(n_keys)[None, :]
            valid = idx >= 0
            kv = jnp.take(kvs[g], jnp.maximum(idx, 0), axis=1)
            s = jnp.einsum('bqhd,bqmhd->bhqm', q_blk[:, :, g], kv[:, :, :, 0]).astype(jnp.float32) * scale
            s = jnp.where(valid[None, None], s, -jnp.inf)
            s_max = s.max(-1, keepdims=True)
            p = jnp.exp(s - s_max)
            den = p.sum(-1)
            o = jnp.einsum('bhqm,bqmhd->bqhd', p, kv[:, :, :, 1].astype(jnp.float32))
            outs.append(o / jnp.swapaxes(den, 1, 2)[..., None])
            lses.append(jnp.swapaxes(s_max[..., 0] + jnp.log(den), 1, 2))
        w = jax.nn.softmax(jnp.stack(lses, axis=0), axis=0)
        out = outs[0] * w[0][..., None]
        for g in range(1, N_GROUPS):
            out = out + outs[g] * w[g][..., None]
        return out.astype(q_blk.dtype)

    if T > Q_BLOCK and T % Q_BLOCK == 0:
        nb = T // Q_BLOCK
        qb = jnp.moveaxis(q.reshape((B, nb, Q_BLOCK) + q.shape[2:]), 1, 0)
        starts = jnp.arange(nb, dtype=jnp.int32) * Q_BLOCK
        ob = lax.map(lambda a: block(a[0], a[1]), (qb, starts))
        return jnp.moveaxis(ob, 0, 1).reshape((B, T) + ob.shape[3:])
    return block(q, 0)


def dsw_mixer(x, pos, past, w_in, w_out):
    B, T, _ = x.shape
    qkv = (x @ w_in).reshape(B, T, 3, N_GROUPS, A_HEADS, A_HEAD_DIM)
    q = rope(qkv[:, :, 0], pos)
    k = rope(qkv[:, :, 1], pos)
    v = qkv[:, :, 2]
    kvs, offs, new_rows = [], [], []
    for g, (window, _) in enumerate(DSW_GROUPS):
        kv_g = jnp.stack([k[:, :, g], v[:, :, g]], axis=2)
        if past is None:
            kvs.append(kv_g)
            offs.append(0)
            new_rows.append(kv_g[:, max(T - window, 0):])
        else:
            kvs.append(jnp.concatenate([past[g].astype(kv_g.dtype), kv_g], axis=1))
            offs.append(past[g].shape[1])
            new_rows.append(kv_g)
    o = dilated_mixture_attn(q, kvs, offs)
    return o.reshape(B, T, A_WIDTH) @ w_out, new_rows


def mlstm_scan(q, k, v, i_pre, log_f, C0, n0, m0, chunk):
    B, T, H, DK = q.shape
    DV = v.shape[-1]
    nc = T // chunk

    def to_chunks(a):
        a = a.astype(jnp.float32).reshape((B, nc, chunk) + a.shape[2:])
        return jnp.swapaxes(jnp.moveaxis(a, 1, 0), 2, 3)

    causal = jnp.tril(jnp.ones((chunk, chunk), dtype=bool))

    def step(carry, xs):
        C, n, m = carry
        qc, kc, vc, ic, fc = xs
        b = jnp.cumsum(fc, axis=-1)
        log_d = jnp.where(causal, b[..., :, None] - b[..., None, :] + ic[..., None, :], -jnp.inf)
        m_t = jnp.maximum(b + m[..., None], log_d.max(-1))
        d = jnp.exp(log_d - m_t[..., None])
        inter = jnp.exp(b + m[..., None] - m_t)
        s = jnp.einsum('bhtk,bhsk->bhts', qc, kc) * d
        num = jnp.einsum('bhts,bhsv->bhtv', s, vc) + inter[..., None] * jnp.einsum('bhvk,bhtk->bhtv', C, qc)
        den = s.sum(-1) + inter * jnp.einsum('bhk,bhtk->bht', n, qc)
        h = num / jnp.maximum(jnp.abs(den), jnp.exp(-m_t))[..., None]
        m_new = m_t[..., -1]
        decay = jnp.exp(b[..., -1:] - b + ic - m_new[..., None])
        g = jnp.exp(b[..., -1] + m - m_new)
        C_new = g[..., None, None] * C + jnp.einsum('bhs,bhsv,bhsk->bhvk', decay, vc, kc)
        n_new = g[..., None] * n + jnp.einsum('bhs,bhsk->bhk', decay, kc)
        return (C_new, n_new, m_new), h

    carry0 = (C0.astype(jnp.float32), n0.astype(jnp.float32), m0.astype(jnp.float32))
    (C, n, m), hs = lax.scan(step, carry0, (to_chunks(q), to_chunks(k), to_chunks(v), to_chunks(i_pre), to_chunks(log_f)))
    h = jnp.moveaxis(jnp.swapaxes(hs, 2, 3), 0, 1).reshape(B, T, H, DV)
    return h, (C, n, m)


def mlstm_mixer(x, conv_buf, C0, n0, m0, w_in, conv_w, conv_b, gate_b, norm_w, w_out):
    B, T, _ = x.shape
    HK = B_HEADS * B_QK_DIM
    HV = B_HEADS * B_V_DIM
    p = x @ w_in
    qk_pre, v, o_pre, g_pre = jnp.split(p, [2 * HK, 2 * HK + HV, 2 * HK + 2 * HV], axis=-1)
    ext = jnp.concatenate([conv_buf.astype(p.dtype), qk_pre], axis=1)
    qk = conv_b + conv_w[0] * ext[:, 0:T]
    for j in range(1, B_CONV):
        qk = qk + conv_w[j] * ext[:, j:j + T]
    qk = jax.nn.silu(qk)
    q = qk[..., :HK].reshape(B, T, B_HEADS, B_QK_DIM)
    k = qk[..., HK:].reshape(B, T, B_HEADS, B_QK_DIM) * (B_QK_DIM ** -0.5)
    v = v.reshape(B, T, B_HEADS, B_V_DIM)
    g = (g_pre.reshape(B, T, 2, B_HEADS) + gate_b).astype(jnp.float32)
    i_pre = g[:, :, 0]
    log_f = jax.nn.log_sigmoid(g[:, :, 1])
    chunk = B_CHUNK if T % B_CHUNK == 0 else T
    h, (C, n, m) = mlstm_scan(q, k, v, i_pre, log_f, C0, n0, m0, chunk)
    mu = h.mean(-1, keepdims=True)
    var = jnp.square(h - mu).mean(-1, keepdims=True)
    hn = ((h - mu) * lax.rsqrt(var + LN_EPS)).reshape(B, T, HV)
    hn = hn * norm_w * jax.nn.sigmoid(o_pre.astype(jnp.float32))
    y = hn.astype(x.dtype) @ w_out
    return y, (ext[:, T:], C, n, m)


def moe_ffn(x, w_r, b_r, w_gu, b_gu, w_dn, b_dn):
    shp = x.shape
    xt = x.reshape(-1, shp[-1])
    logits = (xt @ w_r + b_r).astype(jnp.float32)
    top_v, top_i = lax.top_k(logits, TOP_K)
    top_w = jax.nn.softmax(top_v, axis=-1)
    gate = jnp.einsum('nke,nk->en', jax.nn.one_hot(top_i, N_EXPERTS, dtype=jnp.float32), top_w)

    def expert(acc, prm):
        wgu, bgu, wdn, bdn, ge = prm
        h = xt @ wgu + bgu
        hg = jnp.minimum(h[:, :D_EXPERT], SWIGLU_LIMIT)
        hu = jnp.clip(h[:, D_EXPERT:], -SWIGLU_LIMIT, SWIGLU_LIMIT)
        act = hg * jax.nn.sigmoid(SWIGLU_ALPHA * hg) * (hu + 1.0)
        return acc + (act @ wdn + bdn).astype(jnp.float32) * ge[:, None], None

    acc, _ = lax.scan(expert, jnp.zeros(xt.shape, jnp.float32), (w_gu, b_gu, w_dn, b_dn, gate))
    return acc.astype(x.dtype).reshape(shp)


def setup_inputs(seed: int = 0) -> dict:
    key = jax.random.key(seed)
    ks = iter(jax.random.split(key, 32))

    def nrm(shape, scale):
        return scale * jax.random.normal(next(ks), shape, jnp.float32)

    HK = B_HEADS * B_QK_DIM
    HV = B_HEADS * B_V_DIM
    win = [min(w, PAST_LEN) for (w, _) in DSW_GROUPS]
    f_bias = jnp.linspace(3.0, 6.0, B_HEADS, dtype=jnp.float32)
    b_gate_b = jnp.stack([nrm((N_B_LAYERS, B_HEADS), 0.1), f_bias + nrm((N_B_LAYERS, B_HEADS), 0.1)], axis=1)
    return {
        'x_prompt': nrm((BATCH, SEQ, D_MODEL), 1.0),
        'x_sample': nrm((DEC_BATCH, DEC_SEQ, D_MODEL), 1.0),
        'cache_kv_w128': nrm((N_A_LAYERS, DEC_BATCH, win[0], 2, A_HEADS, A_HEAD_DIM), 1.0),
        'cache_kv_w512': nrm((N_A_LAYERS, DEC_BATCH, win[1], 2, A_HEADS, A_HEAD_DIM), 1.0),
        'cache_kv_w2048': nrm((N_A_LAYERS, DEC_BATCH, win[2], 2, A_HEADS, A_HEAD_DIM), 1.0),
        'state_C': nrm((N_B_LAYERS, DEC_BATCH, B_HEADS, B_V_DIM, B_QK_DIM), 0.1),
        'state_n': nrm((N_B_LAYERS, DEC_BATCH, B_HEADS, B_QK_DIM), 0.1),
        'state_m': nrm((N_B_LAYERS, DEC_BATCH, B_HEADS), 1.0),
        'state_conv': nrm((N_B_LAYERS, DEC_BATCH, B_CONV - 1, 2 * HK), 1.0),
        'a_w_in': nrm((N_A_LAYERS, D_MODEL, 3 * N_GROUPS * A_WIDTH), D_MODEL ** -0.5),
        'a_w_out': nrm((N_A_LAYERS, A_WIDTH, D_MODEL), DN_BETA * A_WIDTH ** -0.5),
        'b_w_in': nrm((N_B_LAYERS, D_MODEL, 2 * HK + 2 * HV + 2 * B_HEADS), D_MODEL ** -0.5),
        'b_conv_w': nrm((N_B_LAYERS, B_CONV, 2 * HK), B_CONV ** -0.5),
        'b_conv_b': nrm((N_B_LAYERS, 2 * HK), 0.01),
        'b_gate_b': b_gate_b,
        'b_norm_w': 1.0 + nrm((N_B_LAYERS, HV), 0.01),
        'b_w_out': nrm((N_B_LAYERS, HV, D_MODEL), DN_BETA * HV ** -0.5),
        'ln_g': 1.0 + nrm((DEPTH, 2, D_MODEL), 0.01),
        'ln_b': nrm((DEPTH, 2, D_MODEL), 0.01),
        'moe_w_router': nrm((DEPTH, D_MODEL, N_EXPERTS), D_MODEL ** -0.5),
        'moe_b_router': nrm((DEPTH, N_EXPERTS), 0.01),
        'moe_w_gu': nrm((DEPTH, N_EXPERTS, D_MODEL, 2 * D_EXPERT), D_MODEL ** -0.5),
        'moe_b_gu': nrm((DEPTH, N_EXPERTS, 2 * D_EXPERT), 0.01),
        'moe_w_down': nrm((DEPTH, N_EXPERTS, D_EXPERT, D_MODEL), DN_BETA * D_EXPERT ** -0.5),
        'moe_b_down': nrm((DEPTH, N_EXPERTS, D_MODEL), 0.01),
    }


def reference(x_prompt, x_sample, cache_kv_w128, cache_kv_w512, cache_kv_w2048, state_C, state_n, state_m, state_conv, a_w_in, a_w_out, b_w_in, b_conv_w, b_conv_b, b_gate_b, b_norm_w, b_w_out, ln_g, ln_b, moe_w_router, moe_b_router, moe_w_gu, moe_b_gu, moe_w_down, moe_b_down):
    Bp, T, _ = x_prompt.shape
    Bs, Ts, _ = x_sample.shape
    HK = B_HEADS * B_QK_DIM
    pos_p = jnp.arange(T, dtype=jnp.int32)
    pos_s = PAST_LEN + jnp.arange(Ts, dtype=jnp.int32)
    kv_caches = (cache_kv_w128, cache_kv_w512, cache_kv_w2048)
    xp, xs = x_prompt, x_sample
    kv_p = [[] for _ in DSW_GROUPS]
    kv_s = [[] for _ in DSW_GROUPS]
    rec_p = [[] for _ in range(4)]
    rec_s = [[] for _ in range(4)]
    for i in range(DEPTH):
        j = i // N_MIXERS
        if i % N_MIXERS == 0:
            yp, rows_p = dsw_mixer(xp, pos_p, None, a_w_in[j], a_w_out[j])
            ys, rows_s = dsw_mixer(xs, pos_s, [c[j] for c in kv_caches], a_w_in[j], a_w_out[j])
            for g in range(N_GROUPS):
                kv_p[g].append(rows_p[g])
                kv_s[g].append(rows_s[g])
        else:
            b_params = (b_w_in[j], b_conv_w[j], b_conv_b[j], b_gate_b[j], b_norm_w[j], b_w_out[j])
            yp, st_p = mlstm_mixer(xp, jnp.zeros((Bp, B_CONV - 1, 2 * HK), xp.dtype), jnp.zeros((Bp, B_HEADS, B_V_DIM, B_QK_DIM), jnp.float32), jnp.zeros((Bp, B_HEADS, B_QK_DIM), jnp.float32), jnp.zeros((Bp, B_HEADS), jnp.float32), *b_params)
            ys, st_s = mlstm_mixer(xs, state_conv[j], state_C[j], state_n[j], state_m[j], *b_params)
            for r in range(4):
                rec_p[r].append(st_p[r])
                rec_s[r].append(st_s[r])
        xp = layer_norm(DN_ALPHA * xp + yp, ln_g[i, 0], ln_b[i, 0])
        xs = layer_norm(DN_ALPHA * xs + ys, ln_g[i, 0], ln_b[i, 0])
        moe_params = (moe_w_router[i], moe_b_router[i], moe_w_gu[i], moe_b_gu[i], moe_w_down[i], moe_b_down[i])
        xp = layer_norm(DN_ALPHA * xp + moe_ffn(xp, *moe_params), ln_g[i, 1], ln_b[i, 1])
        xs = layer_norm(DN_ALPHA * xs + moe_ffn(xs, *moe_params), ln_g[i, 1], ln_b[i, 1])
    kv128_p, kv512_p, kv2048_p = [jnp.stack(r, axis=0) for r in kv_p]
    kv128_s, kv512_s, kv2048_s = [jnp.stack(r, axis=0) for r in kv_s]
    conv_p, C_p, n_p, m_p = [jnp.stack(r, axis=0) for r in rec_p]
    conv_s, C_s, n_s, m_s = [jnp.stack(r, axis=0) for r in rec_s]
    return (xp, xs, kv128_p, kv512_p, kv2048_p, C_p, n_p, m_p, conv_p, kv128_s, kv512_s, kv2048_s, C_s, n_s, m_s, conv_s)
```

```python
import functools

import jax
import jax.numpy as jnp
from jax import lax
from jax.experimental import pallas as pl
from jax.experimental.pallas import tpu as pltpu

F32 = jnp.float32
BF16 = jnp.bfloat16
HIGHEST = lax.Precision.HIGHEST

D_MODEL = 1024
SEQ = 16384
N_SAMPLE = 32
PAST_LEN = 16384
DEPTH = 2
A_HEADS = 16
A_HEAD_DIM = 64
DILATIONS = (1, 4, 16)
WINDOWS = (128, 512, 2048)
ROPE_THETA = 10000.0
B_HEADS = 4
B_QK = 128
B_V = 256
N_EXPERTS = 32
TOP_K = 4
SWIGLU_LIMIT = 7.0
SWIGLU_ALPHA = 1.702
DN_ALPHA = (2 * DEPTH) ** 0.25
LN_EPS = 1e-5

LANES = 128
TM = 256
NT_P = SEQ // TM
NP = SEQ + TM
NT = NP // TM
ATT_TILE = 2048
QB = 128
N_ATT_TILES = SEQ // ATT_TILE
TAIL_T0 = NT_P - ATT_TILE // TM
EXP_TM = 256
N_ASSIGN = NP * TOP_K
N_EXP_TILES = (N_ASSIGN + N_EXPERTS * (EXP_TM - 1)) // EXP_TM + 1
N_SORTED = N_EXP_TILES * EXP_TM
VMEM_LIMIT = 56 * 1024 * 1024


def _cparams(sem):
    return pltpu.CompilerParams(dimension_semantics=sem, vmem_limit_bytes=VMEM_LIMIT)


def _layer_norm(z, g, b):
    mu = jnp.mean(z, axis=-1, keepdims=True)
    zc = z - mu
    var = jnp.mean(zc * zc, axis=-1, keepdims=True)
    return zc * lax.rsqrt(var + LN_EPS) * g + b


def _rope(res, cos, sin, n_lane_tiles):
    rows = res.shape[0]
    lane = lax.broadcasted_iota(jnp.int32, (rows, LANES), 1)
    first = (lane % A_HEAD_DIM) < (A_HEAD_DIM // 2)
    outs = []
    for c in range(n_lane_tiles):
        seg = res[:, c * LANES:(c + 1) * LANES]
        rot = jnp.where(first, pltpu.roll(seg, LANES - 32, 1), pltpu.roll(seg, 32, 1))
        outs.append(seg * cos + rot * sin)
    return outs


def _proj_a_kernel(x_ref, w_ref, cos_ref, sin_ref, o0_ref, o1_ref, o2_ref, scr):
    j = pl.program_id(0)
    xb = x_ref[...].astype(BF16)
    res = jnp.dot(xb, w_ref[...], preferred_element_type=F32)
    n_tiles = 3 * D_MODEL // LANES

    @pl.when(j < 2)
    def _():
        scale = jnp.where(j == 0, A_HEAD_DIM ** -0.5, 1.0).astype(F32)
        segs = _rope(res, cos_ref[...], sin_ref[...], n_tiles)
        for c in range(n_tiles):
            scr[c] = segs[c] * scale

    @pl.when(j == 2)
    def _():
        for c in range(n_tiles):
            scr[c] = res[:, c * LANES:(c + 1) * LANES]

    per_group = D_MODEL // LANES
    for c in range(per_group):
        o0_ref[0, 0, 0, :, c * LANES:(c + 1) * LANES] = scr[c].astype(BF16)
    for g, (d, o_ref) in enumerate(((4, o1_ref), (16, o2_ref)), start=1):
        for r in range(d):
            for c in range(per_group):
                rows = scr[g * per_group + c, pl.ds(r, TM // d, stride=d), :]
                o_ref[0, 0, r, :, c * LANES:(c + 1) * LANES] = rows.astype(BF16)


def _proj_a(x_all, w_bf, cos, sin):
    outs = []
    specs = []
    for d in DILATIONS:
        outs.append(jax.ShapeDtypeStruct((3, N_ATT_TILES, d, ATT_TILE // d, D_MODEL), BF16))
        specs.append(pl.BlockSpec((1, 1, d, TM // d, D_MODEL),
                                  lambda j, m: (j, m // (ATT_TILE // TM), 0, m % (ATT_TILE // TM), 0)))
    return pl.pallas_call(
        _proj_a_kernel,
        out_shape=outs,
        grid=(3, NT_P),
        in_specs=[
            pl.BlockSpec((TM, D_MODEL), lambda j, m: (m, 0)),
            pl.BlockSpec((D_MODEL, 3 * D_MODEL), lambda j, m: (0, j)),
            pl.BlockSpec((TM, LANES), lambda j, m: (m, 0)),
            pl.BlockSpec((TM, LANES), lambda j, m: (m, 0)),
        ],
        out_specs=specs,
        scratch_shapes=[pltpu.VMEM((3 * D_MODEL // LANES, TM, LANES), F32)],
        compiler_params=_cparams(("arbitrary", "arbitrary")),
        name="proj_a",
    )(x_all, w_bf, cos, sin)


def _proj_a_tail_kernel(x_ref, w_ref, cos_ref, sin_ref, o_ref):
    j = pl.program_id(0)
    xb = x_ref[...].astype(BF16)
    res = jnp.dot(xb, w_ref[...], preferred_element_type=F32)
    n_tiles = 3 * D_MODEL // LANES

    @pl.when(j < 2)
    def _():
        scale = jnp.where(j == 0, A_HEAD_DIM ** -0.5, 1.0).astype(F32)
        segs = _rope(res, cos_ref[...], sin_ref[...], n_tiles)
        for c in range(n_tiles):
            o_ref[0, :, c * LANES:(c + 1) * LANES] = segs[c] * scale

    @pl.when(j == 2)
    def _():
        o_ref[0] = res


def _proj_a_tail(x_all, w_bf, cos, sin):
    n = NT - TAIL_T0
    return pl.pallas_call(
        _proj_a_tail_kernel,
        out_shape=jax.ShapeDtypeStruct((3, n * TM, 3 * D_MODEL), F32),
        grid=(3, n),
        in_specs=[
            pl.BlockSpec((TM, D_MODEL), lambda j, m: (m + TAIL_T0, 0)),
            pl.BlockSpec((D_MODEL, 3 * D_MODEL), lambda j, m: (0, j)),
            pl.BlockSpec((TM, LANES), lambda j, m: (m + TAIL_T0, 0)),
            pl.BlockSpec((TM, LANES), lambda j, m: (m + TAIL_T0, 0)),
        ],
        out_specs=pl.BlockSpec((1, TM, 3 * D_MODEL), lambda j, m: (j, m, 0)),
        compiler_params=_cparams(("arbitrary", "arbitrary")),
        name="proj_a_tail",
    )(x_all, w_bf, cos, sin)


def _attn_kernel(*refs):
    in_refs = refs[:15]
    o_ref, oscr, lscr = refs[15:]
    jt = pl.program_id(0)
    lane = lax.broadcasted_iota(jnp.int32, (QB, LANES), 1)
    lo = lane < A_HEAD_DIM
    row = lax.broadcasted_iota(jnp.int32, (QB, 2 * QB), 0)
    col = lax.broadcasted_iota(jnp.int32, (QB, 2 * QB), 1)
    mask_cur = (col >= QB) & (col - QB <= row)
    mask_prev = (col < QB) & (col >= row)
    neg = jnp.float32(-jnp.inf)
    bias_full = jnp.concatenate([jnp.where(mask_cur | mask_prev, 0.0, neg)] * 2, axis=0)
    bias_first = jnp.concatenate([jnp.where(mask_cur, 0.0, neg)] * 2, axis=0)
    n_blocks = ATT_TILE // QB
    zero = jnp.zeros((QB, LANES), BF16)

    for g, d in enumerate(DILATIONS):
        q_ref, kc_ref, kp_ref, vc_ref, vp_ref = in_refs[5 * g:5 * g + 5]
        per_class = n_blocks // d
        for b in range(n_blocks):
            r, c = divmod(b, per_class)
            rows = pl.ds(c * QB, QB)
            q = q_ref[0, 0, r, rows, :]
            kc = kc_ref[0, 0, r, rows, :]
            vc = vc_ref[0, 0, r, rows, :]
            if c == 0:
                prow = pl.ds((per_class - 1) * QB, QB)
                kp = kp_ref[0, 0, r, prow, :]
                vp = vp_ref[0, 0, r, prow, :]
                bias = jnp.where(jt > 0, bias_full, bias_first)
            else:
                prow = pl.ds((c - 1) * QB, QB)
                kp = kc_ref[0, 0, r, prow, :]
                vp = vc_ref[0, 0, r, prow, :]
                bias = bias_full
            lhs = jnp.concatenate([jnp.where(lo, q, zero), jnp.where(lo, zero, q)], axis=0)
            k2 = jnp.concatenate([kp, kc], axis=0)
            v2 = jnp.concatenate([vp, vc], axis=0)
            s = lax.dot_general(lhs, k2, (((1,), (1,)), ((), ())), preferred_element_type=F32)
            s = s + bias
            mx = jnp.max(s, axis=1, keepdims=True)
            p = jnp.exp(s - mx)
            den = jnp.sum(p, axis=1, keepdims=True)
            o2 = jnp.dot(p.astype(BF16), v2, preferred_element_type=F32) / den
            lse = mx + jnp.log(den)
            o = jnp.where(lo, o2[:QB], o2[QB:])
            l = jnp.where(lo, jnp.broadcast_to(lse[:QB], (QB, LANES)),
                          jnp.broadcast_to(lse[QB:], (QB, LANES)))
            if d == 1:
                dst = pl.ds(b * QB, QB)
            else:
                dst = pl.ds(c * QB * d + r, QB, stride=d)
            oscr[g, dst, :] = o
            lscr[g, dst, :] = l

    def merge(i, carry):
        rows = pl.ds(pl.multiple_of(i * QB, QB), QB)
        l0, l1, l2 = lscr[0, rows, :], lscr[1, rows, :], lscr[2, rows, :]
        mx = jnp.maximum(jnp.maximum(l0, l1), l2)
        w0, w1, w2 = jnp.exp(l0 - mx), jnp.exp(l1 - mx), jnp.exp(l2 - mx)
        acc = oscr[0, rows, :] * w0 + oscr[1, rows, :] * w1 + oscr[2, rows, :] * w2
        o_ref[rows, :] = (acc / (w0 + w1 + w2)).astype(o_ref.dtype)
        return carry

    lax.fori_loop(0, n_blocks, merge, 0)


def _attn_prompt(qkv_groups):
    ins, specs = [], []
    for g, d in enumerate(DILATIONS):
        arr = qkv_groups[g]
        blk = (1, 1, d, ATT_TILE // d, LANES)
        cur = lambda s: (lambda j, h: (s, j, 0, 0, h))
        prev = lambda s: (lambda j, h: (s, jnp.maximum(j - 1, 0), 0, 0, h))
        for imap in (cur(0), cur(1), prev(1), cur(2), prev(2)):
            ins.append(arr)
            specs.append(pl.BlockSpec(blk, imap))
    return pl.pallas_call(
        _attn_kernel,
        out_shape=jax.ShapeDtypeStruct((SEQ, D_MODEL), BF16),
        grid=(N_ATT_TILES, D_MODEL // LANES),
        in_specs=specs,
        out_specs=pl.BlockSpec((ATT_TILE, LANES), lambda j, h: (j, h)),
        scratch_shapes=[pltpu.VMEM((3, ATT_TILE, LANES), F32),
                        pltpu.VMEM((3, ATT_TILE, LANES), F32)],
        compiler_params=_cparams(("arbitrary", "arbitrary")),
        name="attn_prompt",
    )(*ins)


def _attn_sample_kernel(qkv_ref, c0_ref, c1_ref, c2_ref, o_ref):
    ones = jnp.ones((A_HEAD_DIM, LANES), F32)

    def head_sum(a):
        return jnp.dot(a, ones, precision=HIGHEST, preferred_element_type=F32)

    outs, lses = [], []
    for g, c_ref in enumerate((c0_ref, c1_ref, c2_ref)):
        q = qkv_ref[0, g]
        kn = qkv_ref[1, g]
        vn = qkv_ref[2, g]
        k = c_ref[:, 0]
        v = c_ref[:, 1]
        n_keys = k.shape[0]
        s = head_sum((k * q[None]).reshape(n_keys * A_HEADS, A_HEAD_DIM))
        s = s.reshape(n_keys, A_HEADS, LANES)
        s_self = head_sum(kn * q)
        mx = jnp.maximum(jnp.max(s, axis=0), s_self)
        p = jnp.exp(s - mx[None])
        p_self = jnp.exp(s_self - mx)
        den = jnp.sum(p, axis=0) + p_self
        acc = jnp.sum(p[:, :, :A_HEAD_DIM] * v, axis=0) + p_self[:, :A_HEAD_DIM] * vn
        outs.append(acc / den[:, :A_HEAD_DIM])
        lses.append((mx + jnp.log(den))[:, :A_HEAD_DIM])
    mx = jnp.maximum(jnp.maximum(lses[0], lses[1]), lses[2])
    ws = [jnp.exp(l - mx) for l in lses]
    acc = outs[0] * ws[0] + outs[1] * ws[1] + outs[2] * ws[2]
    o_ref[...] = acc / (ws[0] + ws[1] + ws[2])


def _attn_sample(qkv_s, caches):
    specs = [pl.BlockSpec((None, 3, 3, A_HEADS, A_HEAD_DIM), lambda b: (b, 0, 0, 0, 0))]
    for c in caches:
        specs.append(pl.BlockSpec((None, c.shape[1], None, 2, A_HEADS, A_HEAD_DIM),
                                  lambda b: (b, 0, 0, 0, 0, 0)))
    return pl.pallas_call(
        _attn_sample_kernel,
        out_shape=jax.ShapeDtypeStruct((N_SAMPLE, A_HEADS, A_HEAD_DIM), F32),
        grid=(N_SAMPLE,),
        in_specs=specs,
        out_specs=pl.BlockSpec((None, A_HEADS, A_HEAD_DIM), lambda b: (b, 0, 0)),
        compiler_params=_cparams(("arbitrary",)),
        name="attn_sample",
    )(qkv_s, *caches)


def _out_proj_ln_kernel(yp_ref, ys_ref, x_ref, w_ref, g_ref, b_ref, o_ref):
    m = pl.program_id(0)
    y = jnp.where(m < NT_P, yp_ref[...], ys_ref[...])
    proj = jnp.dot(y, w_ref[...], preferred_element_type=F32)
    o_ref[...] = _layer_norm(DN_ALPHA * x_ref[...] + proj, g_ref[...], b_ref[...])


def _out_proj_ln(y_p, y_s, x_all, w_bf, g, b):
    return pl.pallas_call(
        _out_proj_ln_kernel,
        out_shape=jax.ShapeDtypeStruct((NP, D_MODEL), F32),
        grid=(NT,),
        in_specs=[
            pl.BlockSpec((TM, D_MODEL), lambda m: (jnp.minimum(m, NT_P - 1), 0)),
            pl.BlockSpec((TM, D_MODEL), lambda m: (0, 0)),
            pl.BlockSpec((TM, D_MODEL), lambda m: (m, 0)),
            pl.BlockSpec((D_MODEL, D_MODEL), lambda m: (0, 0)),
            pl.BlockSpec((1, D_MODEL), lambda m: (0, 0)),
            pl.BlockSpec((1, D_MODEL), lambda m: (0, 0)),
        ],
        out_specs=pl.BlockSpec((TM, D_MODEL), lambda m: (m, 0)),
        compiler_params=_cparams(("arbitrary",)),
        name="out_proj_ln",
    )(y_p, y_s, x_all, w_bf, g, b)


def _router_kernel(x_ref, w_ref, b_ref, ids_ref, gate_ref, rank_ref, cnt_ref, run_scr):
    m = pl.program_id(0)

    @pl.when(m == 0)
    def _():
        run_scr[...] = jnp.zeros_like(run_scr)

    logits = jnp.dot(x_ref[...], w_ref[...], precision=HIGHEST,
                     preferred_element_type=F32) + b_ref[...]
    lane = lax.broadcasted_iota(jnp.int32, (TM, LANES), 1)
    tri = (lax.broadcasted_iota(jnp.int32, (TM, TM), 1)
           < lax.broadcasted_iota(jnp.int32, (TM, TM), 0)).astype(BF16)
    running = run_scr[0:1, :]
    ids = jnp.zeros((TM, LANES), jnp.int32)
    vals = jnp.zeros((TM, LANES), F32)
    ranks = jnp.zeros((TM, LANES), jnp.int32)
    work = logits
    for k in range(TOP_K):
        best = jnp.max(work, axis=1, keepdims=True)
        idx = jnp.min(jnp.where(work == best, lane, LANES), axis=1, keepdims=True)
        sel = lane == idx
        work = jnp.where(sel, -jnp.inf, work)
        onehot = sel.astype(BF16)
        before = jnp.dot(tri, onehot, preferred_element_type=F32) + running
        rank = jnp.sum(jnp.where(sel, before, 0.0), axis=1, keepdims=True)
        running = running + jnp.sum(onehot.astype(F32), axis=0, keepdims=True)
        ids = jnp.where(lane == k, idx, ids)
        vals = jnp.where(lane == k, best, vals)
        ranks = jnp.where(lane == k, rank.astype(jnp.int32), ranks)
    top = lane < TOP_K
    e = jnp.where(top, jnp.exp(vals - vals[:, 0:1]), 0.0)
    ids_ref[...] = ids
    gate_ref[...] = e / jnp.sum(e, axis=1, keepdims=True)
    rank_ref[...] = ranks
    run_scr[...] = jnp.broadcast_to(running, run_scr.shape)
    cnt_ref[...] = jnp.broadcast_to(running, cnt_ref.shape)


def _router(x_all, w_pad, b_pad):
    out = [jax.ShapeDtypeStruct((NP, LANES), jnp.int32),
           jax.ShapeDtypeStruct((NP, LANES), F32),
           jax.ShapeDtypeStruct((NP, LANES), jnp.int32),
           jax.ShapeDtypeStruct((8, LANES), F32)]
    row = pl.BlockSpec((TM, LANES), lambda m: (m, 0))
    return pl.pallas_call(
        _router_kernel,
        out_shape=out,
        grid=(NT,),
        in_specs=[pl.BlockSpec((TM, D_MODEL), lambda m: (m, 0)),
                  pl.BlockSpec((D_MODEL, LANES), lambda m: (0, 0)),
                  pl.BlockSpec((1, LANES), lambda m: (0, 0))],
        out_specs=[row, row, row, pl.BlockSpec((8, LANES), lambda m: (0, 0))],
        scratch_shapes=[pltpu.VMEM((8, LANES), F32)],
        compiler_params=_cparams(("arbitrary",)),
        name="router",
    )(x_all, w_pad, b_pad)


def _gather_rows_kernel(idx_ref, src_ref, o_ref, sem):
    def issue(i, carry):
        pltpu.make_async_copy(src_ref.at[pl.ds(idx_ref[i], 1), :],
                              o_ref.at[pl.ds(i, 1), :], sem).start()
        return carry

    lax.fori_loop(0, EXP_TM, issue, 0)
    pltpu.make_async_copy(src_ref.at[pl.ds(0, EXP_TM), :], o_ref, sem).wait()


def _gather_rows(row_token, x_all):
    return pl.pallas_call(
        _gather_rows_kernel,
        out_shape=jax.ShapeDtypeStruct((N_SORTED, D_MODEL), F32),
        grid=(N_EXP_TILES,),
        in_specs=[pl.BlockSpec((EXP_TM,), lambda i: (i,), memory_space=pltpu.SMEM),
                  pl.BlockSpec(memory_space=pl.ANY)],
        out_specs=pl.BlockSpec((EXP_TM, D_MODEL), lambda i: (i, 0)),
        scratch_shapes=[pltpu.SemaphoreType.DMA],
        compiler_params=_cparams(("arbitrary",)),
        name="moe_dispatch",
    )(row_token, x_all)


def _experts_kernel(te_ref, nv_ref, x_ref, wgu_ref, bgu_ref, wdn_ref, bdn_ref, o_ref,
                    wgu_bf, wdn_bf):
    i = pl.program_id(0)
    prev = te_ref[jnp.maximum(i - 1, 0)]
    fresh = (i == 0) | (te_ref[i] != prev)

    @pl.when(fresh)
    def _():
        wgu_bf[...] = wgu_ref[0].astype(BF16)
        wdn_bf[...] = wdn_ref[0].astype(BF16)

    @pl.when(i < nv_ref[0])
    def _():
        xb = x_ref[...].astype(BF16)
        h = jnp.dot(xb, wgu_bf[...], preferred_element_type=F32) + bgu_ref[0]
        hg = jnp.minimum(h[:, :D_MODEL], SWIGLU_LIMIT)
        hu = jnp.clip(h[:, D_MODEL:], -SWIGLU_LIMIT, SWIGLU_LIMIT)
        act = hg * jax.nn.sigmoid(SWIGLU_ALPHA * hg) * (hu + 1.0)
        o_ref[...] = jnp.dot(act.astype(BF16), wdn_bf[...],
                             preferred_element_type=F32) + bdn_ref[0]

    @pl.when(i >= nv_ref[0])
    def _():
        o_ref[...] = jnp.zeros_like(o_ref)


def _experts(tile_expert, n_valid, x_sorted, w_gu, b_gu, w_dn, b_dn):
    def row_map(i, te, nv):
        return (jnp.minimum(i, nv[0] - 1), 0)

    grid_spec = pltpu.PrefetchScalarGridSpec(
        num_scalar_prefetch=2,
        grid=(N_EXP_TILES,),
        in_specs=[
            pl.BlockSpec((EXP_TM, D_MODEL), row_map),
            pl.BlockSpec((1, D_MODEL, 2 * D_MODEL), lambda i, te, nv: (te[i], 0, 0)),
            pl.BlockSpec((1, 1, 2 * D_MODEL), lambda i, te, nv: (te[i], 0, 0)),
            pl.BlockSpec((1, D_MODEL, D_MODEL), lambda i, te, nv: (te[i], 0, 0)),
            pl.BlockSpec((1, 1, D_MODEL), lambda i, te, nv: (te[i], 0, 0)),
        ],
        out_specs=pl.BlockSpec((EXP_TM, D_MODEL), lambda i, te, nv: (i, 0)),
        scratch_shapes=[pltpu.VMEM((D_MODEL, 2 * D_MODEL), BF16),
                        pltpu.VMEM((D_MODEL, D_MODEL), BF16)],
    )
    return pl.pallas_call(
        _experts_kernel,
        out_shape=jax.ShapeDtypeStruct((N_SORTED, D_MODEL), F32),
        grid_spec=grid_spec,
        compiler_params=_cparams(("arbitrary",)),
        name="moe_experts",
    )(tile_expert, n_valid, x_sorted, w_gu, b_gu, w_dn, b_dn)


CMB_TM = 64


def _combine_ln_kernel(pos_ref, gate_ref, x_ref, g_ref, b_ref, ys_ref, o_ref, buf, sem):
    def issue(t, carry):
        for k in range(TOP_K):
            pltpu.make_async_copy(ys_ref.at[pl.ds(pos_ref[t * TOP_K + k], 1), :],
                                  buf.at[k, pl.ds(t, 1), :], sem).start()
        return carry

    lax.fori_loop(0, CMB_TM, issue, 0)
    pltpu.make_async_copy(buf, buf, sem).wait()
    gate = gate_ref[...]
    acc = buf[0] * gate[:, 0:1]
    for k in range(1, TOP_K):
        acc = acc + buf[k] * gate[:, k:k + 1]
    o_ref[...] = _layer_norm(DN_ALPHA * x_ref[...] + acc, g_ref[...], b_ref[...])


def _combine_ln(pos_flat, gate, x_all, g, b, y_sorted):
    return pl.pallas_call(
        _combine_ln_kernel,
        out_shape=jax.ShapeDtypeStruct((NP, D_MODEL), F32),
        grid=(NP // CMB_TM,),
        in_specs=[pl.BlockSpec((CMB_TM * TOP_K,), lambda m: (m,), memory_space=pltpu.SMEM),
                  pl.BlockSpec((CMB_TM, LANES), lambda m: (m, 0)),
                  pl.BlockSpec((CMB_TM, D_MODEL), lambda m: (m, 0)),
                  pl.BlockSpec((1, D_MODEL), lambda m: (0, 0)),
                  pl.BlockSpec((1, D_MODEL), lambda m: (0, 0)),
                  pl.BlockSpec(memory_space=pl.ANY)],
        out_specs=pl.BlockSpec((CMB_TM, D_MODEL), lambda m: (m, 0)),
        scratch_shapes=[pltpu.VMEM((TOP_K, CMB_TM, D_MODEL), F32), pltpu.SemaphoreType.DMA],
        compiler_params=_cparams(("arbitrary",)),
        name="moe_combine_ln",
    )(pos_flat, gate, x_all, g, b, y_sorted)


def _moe_layer(x_all, w_r, b_r, w_gu, b_gu, w_dn, b_dn, g, b):
    w_pad = jnp.pad(w_r, ((0, 0), (0, LANES - N_EXPERTS)))
    b_pad = jnp.pad(b_r, (0, LANES - N_EXPERTS), constant_values=-jnp.inf)[None]
    ids, gate, rank, counts = _router(x_all, w_pad, b_pad)
    counts = counts[0, :N_EXPERTS].astype(jnp.int32)
    tiles = (counts + EXP_TM - 1) // EXP_TM
    tile_end = jnp.cumsum(tiles)
    start = (tile_end - tiles) * EXP_TM
    ids4 = ids[:, :TOP_K]
    pos = jnp.take(start, ids4) + rank[:, :TOP_K]
    tok = jnp.broadcast_to(jnp.arange(NP, dtype=jnp.int32)[:, None], pos.shape)
    row_token = jnp.zeros((N_SORTED,), jnp.int32).at[pos.reshape(-1)].set(tok.reshape(-1))
    n_valid = tile_end[-1:].astype(jnp.int32)
    tile_ids = jnp.arange(N_EXP_TILES, dtype=jnp.int32)
    tile_expert = jnp.minimum(
        jnp.sum((tile_ids[:, None] >= tile_end[None, :]).astype(jnp.int32), axis=1),
        N_EXPERTS - 1).astype(jnp.int32)
    tile_expert = jnp.where(tile_ids < n_valid[0], tile_expert,
                            jnp.take(tile_expert, jnp.maximum(n_valid[0] - 1, 0)))
    x_sorted = _gather_rows(row_token, x_all)
    y_sorted = _experts(tile_expert, n_valid, x_sorted, w_gu, b_gu[:, None, :], w_dn,
                        b_dn[:, None, :])
    return _combine_ln(pos.reshape(-1), gate, x_all, g[None], b[None], y_sorted)


CONV_K = 4
HK = B_HEADS * B_QK
HV = B_HEADS * B_V


def _conv_silu(prev3, cur, cw, cb):
    rows = cur.shape[0]
    ext = jnp.concatenate([jnp.zeros((5, 2 * HK), F32), prev3, cur], axis=0)
    acc = cb + cw[CONV_K - 1:CONV_K] * cur
    for j in range(CONV_K - 1):
        acc = acc + cw[j:j + 1] * ext[5 + j:5 + j + rows]
    return acc * jax.nn.sigmoid(acc)


def _proj_b_kernel(x_ref, w_ref, wg_ref, gb_ref, cw_ref, cb_ref,
                   q_ref, k_ref, v_ref, op_ref, gt_ref, conv_ref, carry):
    m = pl.program_id(0)

    @pl.when(m == 0)
    def _():
        carry[...] = jnp.zeros_like(carry)

    xb = x_ref[...].astype(BF16)
    p = jnp.dot(xb, w_ref[...], preferred_element_type=F32)
    gates = jnp.dot(xb, wg_ref[...], preferred_element_type=F32) + gb_ref[...]
    lane = lax.broadcasted_iota(jnp.int32, gates.shape, 1)
    is_f = (lane >= B_HEADS) & (lane < 2 * B_HEADS)
    gt_ref[...] = jnp.where(is_f, jax.nn.log_sigmoid(gates), gates)
    qk_pre = p[:, :2 * HK]
    qk = _conv_silu(carry[5:8, :], qk_pre, cw_ref[...], cb_ref[...])
    q_ref[...] = qk[:, :HK].astype(BF16)
    k_ref[...] = (qk[:, HK:] * (B_QK ** -0.5)).astype(BF16)
    v_ref[...] = p[:, 2 * HK:2 * HK + HV].astype(BF16)
    op_ref[...] = p[:, 2 * HK + HV:]
    carry[...] = qk_pre[TM - 8:, :]
    conv_ref[...] = qk_pre[TM - 8:, :]


def _proj_b(x_all, w_bf, wg_bf, gb, cw, cb):
    row = lambda n: pl.BlockSpec((TM, n), lambda m: (m, 0))
    const = lambda s: pl.BlockSpec(s, lambda m: (0, 0))
    out = [jax.ShapeDtypeStruct((SEQ, HK), BF16), jax.ShapeDtypeStruct((SEQ, HK), BF16),
           jax.ShapeDtypeStruct((SEQ, HV), BF16), jax.ShapeDtypeStruct((SEQ, HV), F32),
           jax.ShapeDtypeStruct((SEQ, LANES), F32), jax.ShapeDtypeStruct((8, 2 * HK), F32)]
    return pl.pallas_call(
        _proj_b_kernel,
        out_shape=out,
        grid=(NT_P,),
        in_specs=[row(D_MODEL), const((D_MODEL, 3 * D_MODEL)), const((D_MODEL, LANES)),
                  const((1, LANES)), const((CONV_K, 2 * HK)), const((1, 2 * HK))],
        out_specs=[row(HK), row(HK), row(HV), row(HV), row(LANES), const((8, 2 * HK))],
        scratch_shapes=[pltpu.VMEM((8, 2 * HK), F32)],
        compiler_params=_cparams(("arbitrary",)),
        name="proj_b",
    )(x_all, w_bf, wg_bf, gb, cw, cb)


CHUNK = 128


def _mlstm_kernel(q_ref, k_ref, v_ref, op_ref, gt_ref, nw_ref,
                  h_ref, c_out, n_out, m_out, c_scr, n_scr, m_scr):
    ci = pl.program_id(0)

    @pl.when(ci == 0)
    def _():
        c_scr[...] = jnp.zeros_like(c_scr)
        n_scr[...] = jnp.zeros_like(n_scr)
        m_scr[...] = jnp.zeros_like(m_scr)

    gates = gt_ref[...]
    t_idx = lax.broadcasted_iota(jnp.int32, (CHUNK, CHUNK), 0)
    s_idx = lax.broadcasted_iota(jnp.int32, (CHUNK, CHUNK), 1)
    causal = s_idx <= t_idx
    tri = causal.astype(F32)
    b_cols = jnp.dot(tri, gates, precision=HIGHEST, preferred_element_type=F32)
    gates_t = gates.T
    b_rows = jnp.dot(gates_t, (t_idx <= s_idx).astype(F32), precision=HIGHEST,
                     preferred_element_type=F32)
    for h in range(B_HEADS):
        q = q_ref[:, h * B_QK:(h + 1) * B_QK]
        k = k_ref[:, h * B_QK:(h + 1) * B_QK]
        v = v_ref[:, h * B_V:(h + 1) * B_V]
        m_prev = m_scr[h:h + 1, 0:1]
        n_prev = n_scr[h:h + 1, :]
        c_prev = c_scr[h]
        b_col = b_cols[:, B_HEADS + h:B_HEADS + h + 1]
        i_col = gates[:, h:h + 1]
        b_row = b_rows[B_HEADS + h:B_HEADS + h + 1, :]
        i_row = gates_t[h:h + 1, :]
        log_d = jnp.where(causal, b_col + (i_row - b_row), -jnp.inf)
        m_t = jnp.maximum(b_col + m_prev, jnp.max(log_d, axis=1, keepdims=True))
        dmat = jnp.exp(log_d - m_t)
        inter = jnp.exp(b_col + m_prev - m_t)
        s = lax.dot_general(q, k, (((1,), (1,)), ((), ())), preferred_element_type=F32) * dmat
        qf = q.astype(F32)
        num = (jnp.dot(s.astype(BF16), v, preferred_element_type=F32)
               + inter * jnp.dot(q, c_prev.astype(BF16), preferred_element_type=F32))
        den = (jnp.sum(s, axis=1, keepdims=True)
               + inter * jnp.sum(qf * n_prev, axis=1, keepdims=True))
        hv = num / jnp.maximum(jnp.abs(den), jnp.exp(-m_t))
        m_new = m_t[CHUNK - 1:CHUNK, :]
        b_last = b_col[CHUNK - 1:CHUNK, :]
        decay = jnp.exp(b_last - b_col + i_col - m_new)
        gfac = jnp.exp(b_last + m_prev - m_new)
        kd = k.astype(F32) * decay
        c_scr[h] = gfac * c_prev + jnp.dot(kd.T.astype(BF16), v, preferred_element_type=F32)
        n_scr[h:h + 1, :] = gfac * n_prev + jnp.sum(kd, axis=0, keepdims=True)
        m_scr[h:h + 1, :] = jnp.broadcast_to(m_new, (1, LANES))
        mu = jnp.mean(hv, axis=1, keepdims=True)
        hc = hv - mu
        var = jnp.mean(hc * hc, axis=1, keepdims=True)
        hn = hc * lax.rsqrt(var + LN_EPS)
        cols = slice(h * B_V, (h + 1) * B_V)
        hn = hn * nw_ref[:, cols] * jax.nn.sigmoid(op_ref[:, cols])
        h_ref[:, cols] = hn.astype(h_ref.dtype)
    c_out[...] = c_scr[...]
    n_out[...] = n_scr[...]
    m_out[...] = m_scr[...]


def _mlstm_prompt(q, k, v, o_pre, gates, norm_w):
    row = lambda n: pl.BlockSpec((CHUNK, n), lambda c: (c, 0))
    out = [jax.ShapeDtypeStruct((SEQ, HV), BF16),
           jax.ShapeDtypeStruct((B_HEADS, B_QK, B_V), F32),
           jax.ShapeDtypeStruct((8, LANES), F32),
           jax.ShapeDtypeStruct((8, LANES), F32)]
    return pl.pallas_call(
        _mlstm_kernel,
        out_shape=out,
        grid=(SEQ // CHUNK,),
        in_specs=[row(HK), row(HK), row(HV), row(HV), row(LANES),
                  pl.BlockSpec((1, HV), lambda c: (0, 0))],
        out_specs=[row(HV),
                   pl.BlockSpec((B_HEADS, B_QK, B_V), lambda c: (0, 0, 0)),
                   pl.BlockSpec((8, LANES), lambda c: (0, 0)),
                   pl.BlockSpec((8, LANES), lambda c: (0, 0))],
        scratch_shapes=[pltpu.VMEM((B_HEADS, B_QK, B_V), F32),
                        pltpu.VMEM((8, LANES), F32), pltpu.VMEM((8, LANES), F32)],
        compiler_params=_cparams(("arbitrary",)),
        name="mlstm_prompt",
    )(q, k, v, o_pre, gates, norm_w)


def _mlstm_sample_kernel(x_ref, w_ref, wg_ref, gb_ref, cw_ref, cb_ref, conv_ref,
                         c_ref, n_ref, m_ref, nw_ref,
                         h_ref, conv_out, c_out, n_out, m_out):
    xb = x_ref[...].astype(BF16)
    p = jnp.dot(xb, w_ref[...], preferred_element_type=F32)
    gates = jnp.dot(xb, wg_ref[...], preferred_element_type=F32) + gb_ref[...]
    qk_pre = p[0:1, :2 * HK]
    prev = conv_ref[0]
    cw = cw_ref[...]
    acc = cb_ref[...] + cw[3:4] * qk_pre
    for j in range(CONV_K - 1):
        acc = acc + cw[j:j + 1] * prev[j:j + 1]
    qk = acc * jax.nn.sigmoid(acc)
    conv_out[0] = jnp.concatenate([prev[1:3], qk_pre], axis=0)
    v_all = p[0:1, 2 * HK:2 * HK + HV]
    o_pre = p[0:1, 2 * HK + HV:]
    ones8 = jnp.ones((8, 1), F32)
    for h in range(B_HEADS):
        q = qk[:, h * B_QK:(h + 1) * B_QK]
        k = qk[:, HK + h * B_QK:HK + (h + 1) * B_QK] * (B_QK ** -0.5)
        v = v_all[:, h * B_V:(h + 1) * B_V]
        i_pre = gates[0:1, h:h + 1]
        log_f = jax.nn.log_sigmoid(gates[0:1, B_HEADS + h:B_HEADS + h + 1])
        m_prev = m_ref[0, h:h + 1, 0:1]
        n_prev = n_ref[0, h:h + 1, :]
        c_prev = c_ref[0, h]
        m_t = jnp.maximum(log_f + m_prev, i_pre)
        dgate = jnp.exp(i_pre - m_t)
        inter = jnp.exp(log_f + m_prev - m_t)
        s = jnp.sum(q * k, axis=1, keepdims=True) * dgate
        cq = lax.dot_general(ones8 * q, c_prev, (((1,), (1,)), ((), ())),
                             precision=HIGHEST, preferred_element_type=F32)[0:1]
        num = s * v + inter * cq
        den = s + inter * jnp.sum(n_prev * q, axis=1, keepdims=True)
        hv = num / jnp.maximum(jnp.abs(den), jnp.exp(-m_t))
        v_cols = jnp.broadcast_to(v, (B_QK, B_V)).T
        c_out[0, h] = inter * c_prev + dgate * (v_cols * k)
        n_out[0, h:h + 1, :] = inter * n_prev + dgate * k
        m_out[0, h:h + 1, :] = jnp.broadcast_to(m_t, (1, LANES))
        mu = jnp.mean(hv, axis=1, keepdims=True)
        hc = hv - mu
        var = jnp.mean(hc * hc, axis=1, keepdims=True)
        cols = slice(h * B_V, (h + 1) * B_V)
        hn = hc * lax.rsqrt(var + LN_EPS) * nw_ref[:, cols] * jax.nn.sigmoid(o_pre[:, cols])
        h_ref[0, :, cols] = jnp.broadcast_to(hn, (8, B_V))


def _mlstm_sample(x_s8, w_bf, wg_bf, gb, cw, cb, conv_state, c_state, n_state, m_state, norm_w):
    const = lambda s: pl.BlockSpec(s, lambda b: (0,) * len(s))
    per = lambda s: pl.BlockSpec((1,) + s, lambda b: (b,) + (0,) * len(s))
    out = [jax.ShapeDtypeStruct((N_SAMPLE, 8, HV), F32),
           jax.ShapeDtypeStruct((N_SAMPLE, 3, 2 * HK), F32),
           jax.ShapeDtypeStruct((N_SAMPLE, B_HEADS, B_V, B_QK), F32),
           jax.ShapeDtypeStruct((N_SAMPLE, B_HEADS, B_QK), F32),
           jax.ShapeDtypeStruct((N_SAMPLE, B_HEADS, LANES), F32)]
    return pl.pallas_call(
        _mlstm_sample_kernel,
        out_shape=out,
        grid=(N_SAMPLE,),
        in_specs=[pl.BlockSpec((8, D_MODEL), lambda b: (b, 0)),
                  const((D_MODEL, 3 * D_MODEL)), const((D_MODEL, LANES)), const((1, LANES)),
                  const((CONV_K, 2 * HK)), const((1, 2 * HK)),
                  per((3, 2 * HK)), per((B_HEADS, B_V, B_QK)), per((B_HEADS, B_QK)),
                  per((B_HEADS, LANES)), const((1, HV))],
        out_specs=[per((8, HV)), per((3, 2 * HK)), per((B_HEADS, B_V, B_QK)),
                   per((B_HEADS, B_QK)), per((B_HEADS, LANES))],
        compiler_params=_cparams(("arbitrary",)),
        name="mlstm_sample",
    )(x_s8, w_bf, wg_bf, gb, cw, cb, conv_state, c_state, n_state, m_state, norm_w)


def _rope_tables():
    half = A_HEAD_DIM // 2
    pos = jnp.concatenate([jnp.arange(SEQ, dtype=jnp.int32),
                           jnp.full((TM,), PAST_LEN, jnp.int32)])
    inv = ROPE_THETA ** (-jnp.arange(half, dtype=F32) / half)
    ang = pos.astype(F32)[:, None] * inv[None, :]
    cos, sin = jnp.cos(ang), jnp.sin(ang)
    cos_t = jnp.tile(cos, (1, LANES // half))
    sin_t = jnp.tile(jnp.concatenate([-sin, sin], axis=1), (1, LANES // A_HEAD_DIM))
    return cos_t, sin_t


def kernel(x_prompt, x_sample, cache_kv_w128, cache_kv_w512, cache_kv_w2048, state_C, state_n,
           state_m, state_conv, a_w_in, a_w_out, b_w_in, b_conv_w, b_conv_b, b_gate_b, b_norm_w,
           b_w_out, ln_g, ln_b, moe_w_router, moe_b_router, moe_w_gu, moe_b_gu, moe_w_down,
           moe_b_down):
    pad = jnp.zeros((TM - N_SAMPLE, D_MODEL), F32)
    x_all = jnp.concatenate([x_prompt[0], x_sample[:, 0], pad], axis=0)

    cos_t, sin_t = _rope_tables()
    w_in = a_w_in[0].astype(BF16)
    groups = _proj_a(x_all, w_in, cos_t, sin_t)
    tail = _proj_a_tail(x_all, w_in, cos_t, sin_t)
    o_p = _attn_prompt(groups)
    n_tail = ATT_TILE
    qkv_s = tail[:, n_tail:n_tail + N_SAMPLE].reshape(3, N_SAMPLE, 3, A_HEADS, A_HEAD_DIM)
    qkv_s = jnp.transpose(qkv_s, (1, 0, 2, 3, 4))
    caches = []
    for c, d in zip((cache_kv_w128, cache_kv_w512, cache_kv_w2048), DILATIONS):
        w = c.shape[2]
        caches.append(c[0].reshape(N_SAMPLE, w // d, d, 2, A_HEADS, A_HEAD_DIM))
    o_s = _attn_sample(qkv_s, caches).reshape(N_SAMPLE, D_MODEL)
    o_s = jnp.concatenate([o_s, pad], axis=0).astype(BF16)
    x_all = _out_proj_ln(o_p, o_s, x_all, a_w_out[0].astype(BF16), ln_g[0, 0][None], ln_b[0, 0][None])
    x_all = _moe_layer(x_all, moe_w_router[0], moe_b_router[0], moe_w_gu[0], moe_b_gu[0],
                       moe_w_down[0], moe_b_down[0], ln_g[0, 1], ln_b[0, 1])

    kv_p, kv_s = [], []
    for g, (d, w) in enumerate(zip(DILATIONS, WINDOWS)):
        k_nat = tail[1, n_tail - w:n_tail, g * D_MODEL:(g + 1) * D_MODEL]
        v_nat = tail[2, n_tail - w:n_tail, g * D_MODEL:(g + 1) * D_MODEL]
        kv = jnp.stack([k_nat, v_nat], axis=1).reshape(1, 1, w, 2, A_HEADS, A_HEAD_DIM)
        kv_p.append(kv)
        kv_s.append(jnp.stack([qkv_s[:, 1, g], qkv_s[:, 2, g]], axis=1)[None, :, None])

    w_b = b_w_in[0]
    w_main = w_b[:, :3 * D_MODEL].astype(BF16)
    w_gate = jnp.pad(w_b[:, 3 * D_MODEL:], ((0, 0), (0, LANES - 2 * B_HEADS))).astype(BF16)
    gb = jnp.pad(b_gate_b[0].reshape(-1), (0, LANES - 2 * B_HEADS))[None]
    cw, cb, nw = b_conv_w[0], b_conv_b[0][None], b_norm_w[0][None]
    q, k, v, o_pre, gates, conv_tail = _proj_b(x_all, w_main, w_gate, gb, cw, cb)
    h_p, c_p, n_p, m_p = _mlstm_prompt(q, k, v, o_pre, gates, nw)
    x_s = x_all[SEQ:SEQ + N_SAMPLE]
    x_s8 = jnp.pad(x_s[:, None, :], ((0, 0), (0, 7), (0, 0))).reshape(N_SAMPLE * 8, D_MODEL)
    m_in = jnp.broadcast_to(state_m[0][:, :, None], (N_SAMPLE, B_HEADS, LANES))
    h_s, conv_s, c_s, n_s, m_s = _mlstm_sample(x_s8, w_main, w_gate, gb, cw, cb, state_conv[0],
                                               state_C[0], state_n[0], m_in, nw)
    h_s = jnp.concatenate([h_s[:, 0, :], pad], axis=0).astype(BF16)
    x_all = _out_proj_ln(h_p, h_s, x_all, b_w_out[0].astype(BF16), ln_g[1, 0][None], ln_b[1, 0][None])
    x_all = _moe_layer(x_all, moe_w_router[1], moe_b_router[1], moe_w_gu[1], moe_b_gu[1],
                       moe_w_down[1], moe_b_down[1], ln_g[1, 1], ln_b[1, 1])

    y_prompt = x_all[:SEQ][None]
    y_sample = x_all[SEQ:SEQ + N_SAMPLE][:, None]
    c_p_out = jnp.swapaxes(c_p, 1, 2)[None, None]
    n_p_out = n_p[:B_HEADS][None, None]
    m_p_out = m_p[:B_HEADS, 0][None, None]
    conv_p = conv_tail[5:8][None, None]
    return (y_prompt, y_sample, kv_p[0], kv_p[1], kv_p[2], c_p_out, n_p_out, m_p_out, conv_p,
            kv_s[0], kv_s[1], kv_s[2], c_s[None], n_s[None], m_s[:, :, 0][None], conv_s[None])
```

```python
import functools

import jax
import jax.numpy as jnp
from jax import lax
from jax.experimental import pallas as pl
from jax.experimental.pallas import tpu as pltpu

F32 = jnp.float32
BF16 = jnp.bfloat16
HIGHEST = lax.Precision.HIGHEST

D_MODEL = 1024
SEQ = 16384
N_SAMPLE = 32
PAST_LEN = 16384
DEPTH = 2
A_HEADS = 16
A_HEAD_DIM = 64
DILATIONS = (1, 4, 16)
WINDOWS = (128, 512, 2048)
ROPE_THETA = 10000.0
B_HEADS = 4
B_QK = 128
B_V = 256
N_EXPERTS = 32
TOP_K = 4
SWIGLU_LIMIT = 7.0
SWIGLU_ALPHA = 1.702
DN_ALPHA = (2 * DEPTH) ** 0.25
LN_EPS = 1e-5

LANES = 128
TM = 256
NT_P = SEQ // TM
NP = SEQ + TM
NT = NP // TM
ATT_TILE = 2048
QB = 128
N_ATT_TILES = SEQ // ATT_TILE
TAIL_T0 = NT_P - ATT_TILE // TM
EXP_TM = 256
N_ASSIGN = NP * TOP_K
N_EXP_TILES = (N_ASSIGN + N_EXPERTS * (EXP_TM - 1)) // EXP_TM + 1
N_SORTED = N_EXP_TILES * EXP_TM
VMEM_LIMIT = 56 * 1024 * 1024


def _cparams(sem):
    return pltpu.CompilerParams(dimension_semantics=sem, vmem_limit_bytes=VMEM_LIMIT)


def _layer_norm(z, g, b):
    mu = jnp.mean(z, axis=-1, keepdims=True)
    zc = z - mu
    var = jnp.mean(zc * zc, axis=-1, keepdims=True)
    return zc * lax.rsqrt(var + LN_EPS) * g + b


def _rope(res, cos, sin, n_lane_tiles):
    rows = res.shape[0]
    lane = lax.broadcasted_iota(jnp.int32, (rows, LANES), 1)
    first = (lane % A_HEAD_DIM) < (A_HEAD_DIM // 2)
    outs = []
    for c in range(n_lane_tiles):
        seg = res[:, c * LANES:(c + 1) * LANES]
        rot = jnp.where(first, pltpu.roll(seg, LANES - 32, 1), pltpu.roll(seg, 32, 1))
        outs.append(seg * cos + rot * sin)
    return outs


def _class_major(ref, lead, d):
    parts = []
    for r in range(d):
        rows = slice(None) if d == 1 else pl.ds(r, TM // d, stride=d)
        parts.append(ref[rows, :] if lead is None else ref[lead, rows, :])
    return parts[0] if d == 1 else jnp.concatenate(parts, axis=0)


def _proj_a_kernel(x_ref, w_ref, cos_ref, sin_ref, o0_ref, o1_ref, o2_ref):
    j = pl.program_id(0)
    rotated = j < 2
    scale = jnp.where(j == 0, A_HEAD_DIM ** -0.5, 1.0).astype(F32)
    per_group = D_MODEL // LANES
    for g, (d, o_ref) in enumerate(zip(DILATIONS, (o0_ref, o1_ref, o2_ref))):
        xg = jnp.concatenate([_class_major(x_ref, c, d) for c in range(per_group)], axis=1)
        res = jnp.dot(xg.astype(BF16), w_ref[:, g * D_MODEL:(g + 1) * D_MODEL],
                      preferred_element_type=F32)
        cos = jnp.where(rotated, _class_major(cos_ref, None, d) * scale, 1.0)
        sin = jnp.where(rotated, _class_major(sin_ref, None, d) * scale, 0.0)
        out = jnp.concatenate(_rope(res, cos, sin, per_group), axis=1).astype(BF16)
        n = TM // d
        for r in range(d):
            o_ref[0, 0, r] = out[r * n:(r + 1) * n]


def _proj_a(x_slab, w_bf, cos, sin):
    outs = []
    specs = []
    for d in DILATIONS:
        outs.append(jax.ShapeDtypeStruct((3, N_ATT_TILES, d, ATT_TILE // d, D_MODEL), BF16))
        specs.append(pl.BlockSpec((1, 1, d, TM // d, D_MODEL),
                                  lambda j, m: (j, m // (ATT_TILE // TM), 0, m % (ATT_TILE // TM), 0)))
    return pl.pallas_call(
        _proj_a_kernel,
        out_shape=outs,
        grid=(3, NT_P),
        in_specs=[
            pl.BlockSpec((D_MODEL // LANES, TM, LANES), lambda j, m: (0, m, 0)),
            pl.BlockSpec((D_MODEL, 3 * D_MODEL), lambda j, m: (0, j)),
            pl.BlockSpec((TM, LANES), lambda j, m: (m, 0)),
            pl.BlockSpec((TM, LANES), lambda j, m: (m, 0)),
        ],
        out_specs=specs,
        compiler_params=_cparams(("arbitrary", "arbitrary")),
        name="proj_a",
    )(x_slab, w_bf, cos, sin)


def _proj_a_tail_kernel(x_ref, w_ref, cos_ref, sin_ref, o_ref):
    j = pl.program_id(0)
    xb = x_ref[...].astype(BF16)
    res = jnp.dot(xb, w_ref[...], preferred_element_type=F32)
    n_tiles = 3 * D_MODEL // LANES

    @pl.when(j < 2)
    def _():
        scale = jnp.where(j == 0, A_HEAD_DIM ** -0.5, 1.0).astype(F32)
        segs = _rope(res, cos_ref[...], sin_ref[...], n_tiles)
        for c in range(n_tiles):
            o_ref[0, :, c * LANES:(c + 1) * LANES] = segs[c] * scale

    @pl.when(j == 2)
    def _():
        o_ref[0] = res


def _proj_a_tail(x_all, w_bf, cos, sin):
    n = NT - TAIL_T0
    return pl.pallas_call(
        _proj_a_tail_kernel,
        out_shape=jax.ShapeDtypeStruct((3, n * TM, 3 * D_MODEL), F32),
        grid=(3, n),
        in_specs=[
            pl.BlockSpec((TM, D_MODEL), lambda j, m: (m + TAIL_T0, 0)),
            pl.BlockSpec((D_MODEL, 3 * D_MODEL), lambda j, m: (0, j)),
            pl.BlockSpec((TM, LANES), lambda j, m: (m + TAIL_T0, 0)),
            pl.BlockSpec((TM, LANES), lambda j, m: (m + TAIL_T0, 0)),
        ],
        out_specs=pl.BlockSpec((1, TM, 3 * D_MODEL), lambda j, m: (j, m, 0)),
        compiler_params=_cparams(("arbitrary", "arbitrary")),
        name="proj_a_tail",
    )(x_all, w_bf, cos, sin)


def _attn_kernel(*refs):
    in_refs = refs[:15]
    o_ref, oscr, lscr = refs[15:]
    jt = pl.program_id(0)
    lane = lax.broadcasted_iota(jnp.int32, (QB, LANES), 1)
    lo = lane < A_HEAD_DIM
    row = lax.broadcasted_iota(jnp.int32, (QB, 2 * QB), 0)
    col = lax.broadcasted_iota(jnp.int32, (QB, 2 * QB), 1)
    mask_cur = (col >= QB) & (col - QB <= row)
    mask_prev = (col < QB) & (col >= row)
    neg = jnp.float32(-jnp.inf)
    bias_full = jnp.concatenate([jnp.where(mask_cur | mask_prev, 0.0, neg)] * 2, axis=0)
    bias_first = jnp.concatenate([jnp.where(mask_cur, 0.0, neg)] * 2, axis=0)
    n_blocks = ATT_TILE // QB
    zero = jnp.zeros((QB, LANES), BF16)

    for g, d in enumerate(DILATIONS):
        q_ref, kc_ref, kp_ref, vc_ref, vp_ref = in_refs[5 * g:5 * g + 5]
        per_class = n_blocks // d
        for b in range(n_blocks):
            r, c = divmod(b, per_class)
            rows = pl.ds(c * QB, QB)
            q = q_ref[0, 0, r, rows, :]
            kc = kc_ref[0, 0, r, rows, :]
            vc = vc_ref[0, 0, r, rows, :]
            if c == 0:
                prow = pl.ds((per_class - 1) * QB, QB)
                kp = kp_ref[0, 0, r, prow, :]
                vp = vp_ref[0, 0, r, prow, :]
                bias = jnp.where(jt > 0, bias_full, bias_first)
            else:
                prow = pl.ds((c - 1) * QB, QB)
                kp = kc_ref[0, 0, r, prow, :]
                vp = vc_ref[0, 0, r, prow, :]
                bias = bias_full
            lhs = jnp.concatenate([jnp.where(lo, q, zero), jnp.where(lo, zero, q)], axis=0)
            k2 = jnp.concatenate([kp, kc], axis=0)
            v2 = jnp.concatenate([vp, vc], axis=0)
            s = lax.dot_general(lhs, k2, (((1,), (1,)), ((), ())), preferred_element_type=F32)
            s = s + bias
            mx = jnp.max(s, axis=1, keepdims=True)
            p = jnp.exp(s - mx)
            den = jnp.sum(p, axis=1, keepdims=True)
            o2 = jnp.dot(p.astype(BF16), v2, preferred_element_type=F32) / den
            lse = mx + jnp.log(den)
            o = jnp.where(lo, o2[:QB], o2[QB:])
            l = jnp.where(lo, jnp.broadcast_to(lse[:QB], (QB, LANES)),
                          jnp.broadcast_to(lse[QB:], (QB, LANES)))
            if d == 1:
                dst = pl.ds(b * QB, QB)
            else:
                dst = pl.ds(c * QB * d + r, QB, stride=d)
            oscr[g, dst, :] = o
            lscr[g, dst, :] = l

    def merge(i, carry):
        rows = pl.ds(pl.multiple_of(i * QB, QB), QB)
        l0, l1, l2 = lscr[0, rows, :], lscr[1, rows, :], lscr[2, rows, :]
        mx = jnp.maximum(jnp.maximum(l0, l1), l2)
        w0, w1, w2 = jnp.exp(l0 - mx), jnp.exp(l1 - mx), jnp.exp(l2 - mx)
        acc = oscr[0, rows, :] * w0 + oscr[1, rows, :] * w1 + oscr[2, rows, :] * w2
        o_ref[rows, :] = (acc / (w0 + w1 + w2)).astype(o_ref.dtype)
        return carry

    lax.fori_loop(0, n_blocks, merge, 0)


def _attn_prompt(qkv_groups):
    ins, specs = [], []
    for g, d in enumerate(DILATIONS):
        arr = qkv_groups[g]
        blk = (1, 1, d, ATT_TILE // d, LANES)
        cur = lambda s: (lambda j, h: (s, j, 0, 0, h))
        prev = lambda s: (lambda j, h: (s, jnp.maximum(j - 1, 0), 0, 0, h))
        for imap in (cur(0), cur(1), prev(1), cur(2), prev(2)):
            ins.append(arr)
            specs.append(pl.BlockSpec(blk, imap))
    return pl.pallas_call(
        _attn_kernel,
        out_shape=jax.ShapeDtypeStruct((SEQ, D_MODEL), BF16),
        grid=(N_ATT_TILES, D_MODEL // LANES),
        in_specs=specs,
        out_specs=pl.BlockSpec((ATT_TILE, LANES), lambda j, h: (j, h)),
        scratch_shapes=[pltpu.VMEM((3, ATT_TILE, LANES), F32),
                        pltpu.VMEM((3, ATT_TILE, LANES), F32)],
        compiler_params=_cparams(("arbitrary", "arbitrary")),
        name="attn_prompt",
    )(*ins)


N_CACHE_KEYS = WINDOWS[0] // DILATIONS[0]


def _attn_sample_kernel(qkv_ref, c0_ref, c1_ref, c2_ref, o_ref, buf, sem):
    b = pl.program_id(0)
    c_refs = (c0_ref, c1_ref, c2_ref)

    def fetch(sample, slot):
        def issue(m, carry):
            for g, d in enumerate(DILATIONS):
                pltpu.make_async_copy(c_refs[g].at[0, sample, m * d], buf.at[slot, g, m],
                                      sem.at[slot]).start()
            return carry

        lax.fori_loop(0, N_CACHE_KEYS, issue, 0, unroll=4)

    @pl.when(b == 0)
    def _():
        fetch(0, 0)

    @pl.when(b + 1 < N_SAMPLE)
    def _():
        fetch(b + 1, (b + 1) % 2)

    slot = b % 2
    pltpu.make_async_copy(buf.at[slot], buf.at[slot], sem.at[slot]).wait()

    ones = jnp.ones((A_HEAD_DIM, LANES), F32)

    def head_sum(a):
        return jnp.dot(a, ones, precision=HIGHEST, preferred_element_type=F32)

    outs, lses = [], []
    for g in range(len(DILATIONS)):
        q = qkv_ref[0, g]
        kn = qkv_ref[1, g]
        vn = qkv_ref[2, g]
        k = buf[slot, g, :, 0]
        v = buf[slot, g, :, 1]
        n_keys = N_CACHE_KEYS
        s = head_sum((k * q[None]).reshape(n_keys * A_HEADS, A_HEAD_DIM))
        s = s.reshape(n_keys, A_HEADS, LANES)
        s_self = head_sum(kn * q)
        mx = jnp.maximum(jnp.max(s, axis=0), s_self)
        p = jnp.exp(s - mx[None])
        p_self = jnp.exp(s_self - mx)
        den = jnp.sum(p, axis=0) + p_self
        acc = jnp.sum(p[:, :, :A_HEAD_DIM] * v, axis=0) + p_self[:, :A_HEAD_DIM] * vn
        outs.append(acc / den[:, :A_HEAD_DIM])
        lses.append((mx + jnp.log(den))[:, :A_HEAD_DIM])
    mx = jnp.maximum(jnp.maximum(lses[0], lses[1]), lses[2])
    ws = [jnp.exp(l - mx) for l in lses]
    acc = outs[0] * ws[0] + outs[1] * ws[1] + outs[2] * ws[2]
    o_ref[...] = acc / (ws[0] + ws[1] + ws[2])


def _attn_sample(qkv_s, caches):
    specs = [pl.BlockSpec((None, 3, 3, A_HEADS, A_HEAD_DIM), lambda b: (b, 0, 0, 0, 0))]
    specs += [pl.BlockSpec(memory_space=pl.ANY)] * len(caches)
    return pl.pallas_call(
        _attn_sample_kernel,
        out_shape=jax.ShapeDtypeStruct((N_SAMPLE, A_HEADS, A_HEAD_DIM), F32),
        grid=(N_SAMPLE,),
        in_specs=specs,
        out_specs=pl.BlockSpec((None, A_HEADS, A_HEAD_DIM), lambda b: (b, 0, 0)),
        scratch_shapes=[pltpu.VMEM((2, len(DILATIONS), N_CACHE_KEYS, 2, A_HEADS, A_HEAD_DIM), F32),
                        pltpu.SemaphoreType.DMA((2,))],
        compiler_params=_cparams(("arbitrary",)),
        name="attn_sample",
    )(qkv_s, *caches)


def _out_proj_ln_kernel(yp_ref, ys_ref, x_ref, w_ref, g_ref, b_ref, o_ref):
    m = pl.program_id(0)
    y = jnp.where(m < NT_P, yp_ref[...], ys_ref[...])
    proj = jnp.dot(y, w_ref[...], preferred_element_type=F32)
    o_ref[...] = _layer_norm(DN_ALPHA * x_ref[...] + proj, g_ref[...], b_ref[...])


def _out_proj_ln(y_p, y_s, x_all, w_bf, g, b):
    return pl.pallas_call(
        _out_proj_ln_kernel,
        out_shape=jax.ShapeDtypeStruct((NP, D_MODEL), F32),
        grid=(NT,),
        in_specs=[
            pl.BlockSpec((TM, D_MODEL), lambda m: (jnp.minimum(m, NT_P - 1), 0)),
            pl.BlockSpec((TM, D_MODEL), lambda m: (0, 0)),
            pl.BlockSpec((TM, D_MODEL), lambda m: (m, 0)),
            pl.BlockSpec((D_MODEL, D_MODEL), lambda m: (0, 0)),
            pl.BlockSpec((1, D_MODEL), lambda m: (0, 0)),
            pl.BlockSpec((1, D_MODEL), lambda m: (0, 0)),
        ],
        out_specs=pl.BlockSpec((TM, D_MODEL), lambda m: (m, 0)),
        compiler_params=_cparams(("arbitrary",)),
        name="out_proj_ln",
    )(y_p, y_s, x_all, w_bf, g, b)


def _router_kernel(x_ref, w_ref, b_ref, ids_ref, gate_ref, rank_ref, cnt_ref, run_scr):
    m = pl.program_id(0)

    @pl.when(m == 0)
    def _():
        run_scr[...] = jnp.zeros_like(run_scr)

    logits = jnp.dot(x_ref[...], w_ref[...], precision=HIGHEST,
                     preferred_element_type=F32) + b_ref[...]
    lane = lax.broadcasted_iota(jnp.int32, (TM, LANES), 1)
    tri = (lax.broadcasted_iota(jnp.int32, (TM, TM), 1)
           < lax.broadcasted_iota(jnp.int32, (TM, TM), 0)).astype(BF16)
    running = run_scr[0:1, :]
    ids = jnp.zeros((TM, LANES), jnp.int32)
    vals = jnp.zeros((TM, LANES), F32)
    ranks = jnp.zeros((TM, LANES), jnp.int32)
    work = logits
    for k in range(TOP_K):
        best = jnp.max(work, axis=1, keepdims=True)
        idx = jnp.min(jnp.where(work == best, lane, LANES), axis=1, keepdims=True)
        sel = lane == idx
        work = jnp.where(sel, -jnp.inf, work)
        onehot = sel.astype(BF16)
        before = jnp.dot(tri, onehot, preferred_element_type=F32) + running
        rank = jnp.sum(jnp.where(sel, before, 0.0), axis=1, keepdims=True)
        running = running + jnp.sum(onehot.astype(F32), axis=0, keepdims=True)
        ids = jnp.where(lane == k, idx, ids)
        vals = jnp.where(lane == k, best, vals)
        ranks = jnp.where(lane == k, rank.astype(jnp.int32), ranks)
    top = lane < TOP_K
    e = jnp.where(top, jnp.exp(vals - vals[:, 0:1]), 0.0)
    ids_ref[...] = ids
    gate_ref[...] = e / jnp.sum(e, axis=1, keepdims=True)
    rank_ref[...] = ranks
    run_scr[...] = jnp.broadcast_to(running, run_scr.shape)
    cnt_ref[...] = jnp.broadcast_to(running, cnt_ref.shape)


def _router(x_all, w_pad, b_pad):
    out = [jax.ShapeDtypeStruct((NP, LANES), jnp.int32),
           jax.ShapeDtypeStruct((NP, LANES), F32),
           jax.ShapeDtypeStruct((NP, LANES), jnp.int32),
           jax.ShapeDtypeStruct((8, LANES), F32)]
    row = pl.BlockSpec((TM, LANES), lambda m: (m, 0))
    return pl.pallas_call(
        _router_kernel,
        out_shape=out,
        grid=(NT,),
        in_specs=[pl.BlockSpec((TM, D_MODEL), lambda m: (m, 0)),
                  pl.BlockSpec((D_MODEL, LANES), lambda m: (0, 0)),
                  pl.BlockSpec((1, LANES), lambda m: (0, 0))],
        out_specs=[row, row, row, pl.BlockSpec((8, LANES), lambda m: (0, 0))],
        scratch_shapes=[pltpu.VMEM((8, LANES), F32)],
        compiler_params=_cparams(("arbitrary",)),
        name="router",
    )(x_all, w_pad, b_pad)


def _dispatch_kernel(ltile_ref, nv_ref, pos_ref, x_ref, xs_ref, zbuf, sem, zsem):
    m = pl.program_id(0)

    @pl.when(m == 0)
    def _():
        zbuf[...] = jnp.zeros_like(zbuf)

        def zero_copy(tile):
            row = pl.multiple_of(tile * EXP_TM, EXP_TM)
            return pltpu.make_async_copy(zbuf, xs_ref.at[pl.ds(row, EXP_TM), :], zsem)

        for e in range(N_EXPERTS):
            @pl.when(ltile_ref[e] >= 0)
            def _():
                zero_copy(jnp.maximum(ltile_ref[e], 0)).start()

        def start_tail(i, carry):
            zero_copy(i).start()
            return carry

        def wait_tail(i, carry):
            zero_copy(i).wait()
            return carry

        lax.fori_loop(nv_ref[0], N_EXP_TILES, start_tail, 0)
        for e in range(N_EXPERTS):
            @pl.when(ltile_ref[e] >= 0)
            def _():
                zero_copy(jnp.maximum(ltile_ref[e], 0)).wait()
        lax.fori_loop(nv_ref[0], N_EXP_TILES, wait_tail, 0)

    def issue(t, carry):
        for k in range(TOP_K):
            pltpu.make_async_copy(x_ref.at[pl.ds(t, 1), :],
                                  xs_ref.at[pl.ds(pos_ref[t * TOP_K + k], 1), :],
                                  sem).start(priority=k % 2)
        return carry

    lax.fori_loop(0, TM, issue, 0, unroll=8)
    for k in range(TOP_K):
        pltpu.make_async_copy(x_ref, xs_ref.at[pl.ds(0, TM), :], sem).wait()


def _dispatch(last_tile, n_valid, pos_flat, x_all):
    grid_spec = pltpu.PrefetchScalarGridSpec(
        num_scalar_prefetch=2,
        grid=(NT,),
        in_specs=[pl.BlockSpec((TM * TOP_K,), lambda m, lt, nv: (m,), memory_space=pltpu.SMEM),
                  pl.BlockSpec((TM, D_MODEL), lambda m, lt, nv: (m, 0))],
        out_specs=pl.BlockSpec(memory_space=pl.ANY),
        scratch_shapes=[pltpu.VMEM((EXP_TM, D_MODEL), F32),
                        pltpu.SemaphoreType.DMA, pltpu.SemaphoreType.DMA],
    )
    return pl.pallas_call(
        _dispatch_kernel,
        out_shape=jax.ShapeDtypeStruct((N_SORTED, D_MODEL), F32),
        grid_spec=grid_spec,
        compiler_params=_cparams(("arbitrary",)),
        name="moe_dispatch",
    )(last_tile, n_valid, pos_flat, x_all)


def _experts_kernel(te_ref, nv_ref, x_ref, wgu_ref, bgu_ref, wdn_ref, bdn_ref, o_ref,
                    wgu_bf, wdn_bf):
    i = pl.program_id(0)
    prev = te_ref[jnp.maximum(i - 1, 0)]
    fresh = (i == 0) | (te_ref[i] != prev)

    @pl.when(fresh)
    def _():
        wgu_bf[...] = wgu_ref[0].astype(BF16)
        wdn_bf[...] = wdn_ref[0].astype(BF16)

    @pl.when(i < nv_ref[0])
    def _():
        xb = x_ref[...].astype(BF16)
        h = jnp.dot(xb, wgu_bf[...], preferred_element_type=F32) + bgu_ref[0]
        hg = jnp.minimum(h[:, :D_MODEL], SWIGLU_LIMIT)
        hu = jnp.clip(h[:, D_MODEL:], -SWIGLU_LIMIT, SWIGLU_LIMIT)
        act = hg * jax.nn.sigmoid(SWIGLU_ALPHA * hg) * (hu + 1.0)
        o_ref[...] = jnp.dot(act.astype(BF16), wdn_bf[...],
                             preferred_element_type=F32) + bdn_ref[0]

    @pl.when(i >= nv_ref[0])
    def _():
        o_ref[...] = jnp.zeros_like(o_ref)


def _experts(tile_expert, n_valid, x_sorted, w_gu, b_gu, w_dn, b_dn):
    def row_map(i, te, nv):
        return (jnp.minimum(i, nv[0] - 1), 0)

    grid_spec = pltpu.PrefetchScalarGridSpec(
        num_scalar_prefetch=2,
        grid=(N_EXP_TILES,),
        in_specs=[
            pl.BlockSpec((EXP_TM, D_MODEL), row_map),
            pl.BlockSpec((1, D_MODEL, 2 * D_MODEL), lambda i, te, nv: (te[i], 0, 0)),
            pl.BlockSpec((1, 1, 2 * D_MODEL), lambda i, te, nv: (te[i], 0, 0)),
            pl.BlockSpec((1, D_MODEL, D_MODEL), lambda i, te, nv: (te[i], 0, 0)),
            pl.BlockSpec((1, 1, D_MODEL), lambda i, te, nv: (te[i], 0, 0)),
        ],
        out_specs=pl.BlockSpec((EXP_TM, D_MODEL), lambda i, te, nv: (i, 0)),
        scratch_shapes=[pltpu.VMEM((D_MODEL, 2 * D_MODEL), BF16),
                        pltpu.VMEM((D_MODEL, D_MODEL), BF16)],
    )
    return pl.pallas_call(
        _experts_kernel,
        out_shape=jax.ShapeDtypeStruct((N_SORTED, D_MODEL), F32),
        grid_spec=grid_spec,
        compiler_params=_cparams(("arbitrary",)),
        name="moe_experts",
    )(tile_expert, n_valid, x_sorted, w_gu, b_gu, w_dn, b_dn)


CMB_TM = 128
N_CMB = NP // CMB_TM


def _combine_ln_kernel(pos_ref, nxt_ref, gate_ref, x_ref, g_ref, b_ref, ys_ref, o_ref, buf, sem):
    m = pl.program_id(0)

    def fetch(idx_ref, slot):
        def issue(t, carry):
            for k in range(TOP_K):
                pltpu.make_async_copy(ys_ref.at[pl.ds(idx_ref[t * TOP_K + k], 1), :],
                                      buf.at[slot, k, pl.ds(t, 1), :],
                                      sem.at[slot]).start(priority=k % 2)
            return carry

        lax.fori_loop(0, CMB_TM, issue, 0, unroll=8)

    @pl.when(m == 0)
    def _():
        fetch(pos_ref, 0)

    @pl.when(m + 1 < N_CMB)
    def _():
        fetch(nxt_ref, (m + 1) % 2)

    slot = m % 2
    pltpu.make_async_copy(buf.at[slot], buf.at[slot], sem.at[slot]).wait()
    gate = gate_ref[...]
    acc = buf[slot, 0] * gate[:, 0:1]
    for k in range(1, TOP_K):
        acc = acc + buf[slot, k] * gate[:, k:k + 1]
    o_ref[...] = _layer_norm(DN_ALPHA * x_ref[...] + acc, g_ref[...], b_ref[...])


def _combine_ln(pos_flat, gate, x_all, g, b, y_sorted):
    idx = lambda off: pl.BlockSpec((CMB_TM * TOP_K,),
                                   lambda m: (jnp.minimum(m + off, N_CMB - 1),),
                                   memory_space=pltpu.SMEM)
    return pl.pallas_call(
        _combine_ln_kernel,
        out_shape=jax.ShapeDtypeStruct((NP, D_MODEL), F32),
        grid=(N_CMB,),
        in_specs=[idx(0), idx(1),
                  pl.BlockSpec((CMB_TM, LANES), lambda m: (m, 0)),
                  pl.BlockSpec((CMB_TM, D_MODEL), lambda m: (m, 0)),
                  pl.BlockSpec((1, D_MODEL), lambda m: (0, 0)),
                  pl.BlockSpec((1, D_MODEL), lambda m: (0, 0)),
                  pl.BlockSpec(memory_space=pl.ANY)],
        out_specs=pl.BlockSpec((CMB_TM, D_MODEL), lambda m: (m, 0)),
        scratch_shapes=[pltpu.VMEM((2, TOP_K, CMB_TM, D_MODEL), F32),
                        pltpu.SemaphoreType.DMA((2,))],
        compiler_params=_cparams(("arbitrary",)),
        name="moe_combine_ln",
    )(pos_flat, pos_flat, gate, x_all, g, b, y_sorted)


def _moe_layer(x_all, w_r, b_r, w_gu, b_gu, w_dn, b_dn, g, b):
    w_pad = jnp.pad(w_r, ((0, 0), (0, LANES - N_EXPERTS)))
    b_pad = jnp.pad(b_r, (0, LANES - N_EXPERTS), constant_values=-jnp.inf)[None]
    ids, gate, rank, counts = _router(x_all, w_pad, b_pad)
    counts = counts[0, :N_EXPERTS].astype(jnp.int32)
    tiles = (counts + EXP_TM - 1) // EXP_TM
    tile_end = jnp.cumsum(tiles)
    start = (tile_end - tiles) * EXP_TM
    ids4 = ids[:, :TOP_K]
    pos = jnp.take(start, ids4) + rank[:, :TOP_K]
    last_tile = jnp.where(tiles > 0, tile_end - 1, -1).astype(jnp.int32)
    n_valid = tile_end[-1:].astype(jnp.int32)
    tile_ids = jnp.arange(N_EXP_TILES, dtype=jnp.int32)
    tile_expert = jnp.minimum(
        jnp.sum((tile_ids[:, None] >= tile_end[None, :]).astype(jnp.int32), axis=1),
        N_EXPERTS - 1).astype(jnp.int32)
    tile_expert = jnp.where(tile_ids < n_valid[0], tile_expert,
                            jnp.take(tile_expert, jnp.maximum(n_valid[0] - 1, 0)))
    pos_flat = pos.reshape(-1)
    x_sorted = _dispatch(last_tile, n_valid, pos_flat, x_all)
    y_sorted = _experts(tile_expert, n_valid, x_sorted, w_gu, b_gu[:, None, :], w_dn,
                        b_dn[:, None, :])
    return _combine_ln(pos_flat, gate, x_all, g[None], b[None], y_sorted)


CONV_K = 4
HK = B_HEADS * B_QK
HV = B_HEADS * B_V


def _conv_silu(prev3, cur, cw, cb):
    rows = cur.shape[0]
    ext = jnp.concatenate([jnp.zeros((5, 2 * HK), F32), prev3, cur], axis=0)
    acc = cb + cw[CONV_K - 1:CONV_K] * cur
    for j in range(CONV_K - 1):
        acc = acc + cw[j:j + 1] * ext[5 + j:5 + j + rows]
    return acc * jax.nn.sigmoid(acc)


def _proj_b_kernel(x_ref, w_ref, wg_ref, gb_ref, cw_ref, cb_ref,
                   q_ref, k_ref, v_ref, op_ref, gt_ref, conv_ref, carry):
    m = pl.program_id(0)

    @pl.when(m == 0)
    def _():
        carry[...] = jnp.zeros_like(carry)

    xb = x_ref[...].astype(BF16)
    p = jnp.dot(xb, w_ref[...], preferred_element_type=F32)
    gates = jnp.dot(xb, wg_ref[...], preferred_element_type=F32) + gb_ref[...]
    lane = lax.broadcasted_iota(jnp.int32, gates.shape, 1)
    is_f = (lane >= B_HEADS) & (lane < 2 * B_HEADS)
    gt_ref[...] = jnp.where(is_f, jax.nn.log_sigmoid(gates), gates)
    qk_pre = p[:, :2 * HK]
    qk = _conv_silu(carry[5:8, :], qk_pre, cw_ref[...], cb_ref[...])
    q_ref[...] = qk[:, :HK].astype(BF16)
    k_ref[...] = (qk[:, HK:] * (B_QK ** -0.5)).astype(BF16)
    v_ref[...] = p[:, 2 * HK:2 * HK + HV].astype(BF16)
    op_ref[...] = p[:, 2 * HK + HV:]
    carry[...] = qk_pre[TM - 8:, :]
    conv_ref[...] = qk_pre[TM - 8:, :]


def _proj_b(x_all, w_bf, wg_bf, gb, cw, cb):
    row = lambda n: pl.BlockSpec((TM, n), lambda m: (m, 0))
    const = lambda s: pl.BlockSpec(s, lambda m: (0, 0))
    out = [jax.ShapeDtypeStruct((SEQ, HK), BF16), jax.ShapeDtypeStruct((SEQ, HK), BF16),
           jax.ShapeDtypeStruct((SEQ, HV), BF16), jax.ShapeDtypeStruct((SEQ, HV), F32),
           jax.ShapeDtypeStruct((SEQ, LANES), F32), jax.ShapeDtypeStruct((8, 2 * HK), F32)]
    return pl.pallas_call(
        _proj_b_kernel,
        out_shape=out,
        grid=(NT_P,),
        in_specs=[row(D_MODEL), const((D_MODEL, 3 * D_MODEL)), const((D_MODEL, LANES)),
                  const((1, LANES)), const((CONV_K, 2 * HK)), const((1, 2 * HK))],
        out_specs=[row(HK), row(HK), row(HV), row(HV), row(LANES), const((8, 2 * HK))],
        scratch_shapes=[pltpu.VMEM((8, 2 * HK), F32)],
        compiler_params=_cparams(("arbitrary",)),
        name="proj_b",
    )(x_all, w_bf, wg_bf, gb, cw, cb)


CHUNK = 128


def _mlstm_kernel(q_ref, k_ref, v_ref, op_ref, gt_ref, nw_ref,
                  h_ref, c_out, n_out, m_out, c_scr, n_scr, m_scr):
    ci = pl.program_id(0)

    @pl.when(ci == 0)
    def _():
        c_scr[...] = jnp.zeros_like(c_scr)
        n_scr[...] = jnp.zeros_like(n_scr)
        m_scr[...] = jnp.zeros_like(m_scr)

    gates = gt_ref[...]
    t_idx = lax.broadcasted_iota(jnp.int32, (CHUNK, CHUNK), 0)
    s_idx = lax.broadcasted_iota(jnp.int32, (CHUNK, CHUNK), 1)
    causal = s_idx <= t_idx
    tri = causal.astype(F32)
    b_cols = jnp.dot(tri, gates, precision=HIGHEST, preferred_element_type=F32)
    gates_t = gates.T
    b_rows = jnp.dot(gates_t, (t_idx <= s_idx).astype(F32), precision=HIGHEST,
                     preferred_element_type=F32)
    for h in range(B_HEADS):
        q = q_ref[:, h * B_QK:(h + 1) * B_QK]
        k = k_ref[:, h * B_QK:(h + 1) * B_QK]
        v = v_ref[:, h * B_V:(h + 1) * B_V]
        m_prev = m_scr[h:h + 1, 0:1]
        n_prev = n_scr[h:h + 1, :]
        c_prev = c_scr[h]
        b_col = b_cols[:, B_HEADS + h:B_HEADS + h + 1]
        i_col = gates[:, h:h + 1]
        b_row = b_rows[B_HEADS + h:B_HEADS + h + 1, :]
        i_row = gates_t[h:h + 1, :]
        log_d = jnp.where(causal, b_col + (i_row - b_row), -jnp.inf)
        m_t = jnp.maximum(b_col + m_prev, jnp.max(log_d, axis=1, keepdims=True))
        dmat = jnp.exp(log_d - m_t)
        inter = jnp.exp(b_col + m_prev - m_t)
        s = lax.dot_general(q, k, (((1,), (1,)), ((), ())), preferred_element_type=F32) * dmat
        qf = q.astype(F32)
        num = (jnp.dot(s.astype(BF16), v, preferred_element_type=F32)
               + inter * jnp.dot(q, c_prev.astype(BF16), preferred_element_type=F32))
        den = (jnp.sum(s, axis=1, keepdims=True)
               + inter * jnp.sum(qf * n_prev, axis=1, keepdims=True))
        hv = num / jnp.maximum(jnp.abs(den), jnp.exp(-m_t))
        m_new = m_t[CHUNK - 1:CHUNK, :]
        b_last = b_col[CHUNK - 1:CHUNK, :]
        decay = jnp.exp(b_last - b_col + i_col - m_new)
        gfac = jnp.exp(b_last + m_prev - m_new)
        kd = k.astype(F32) * decay
        c_scr[h] = gfac * c_prev + jnp.dot(kd.T.astype(BF16), v, preferred_element_type=F32)
        n_scr[h:h + 1, :] = gfac * n_prev + jnp.sum(kd, axis=0, keepdims=True)
        m_scr[h:h + 1, :] = jnp.broadcast_to(m_new, (1, LANES))
        mu = jnp.mean(hv, axis=1, keepdims=True)
        hc = hv - mu
        var = jnp.mean(hc * hc, axis=1, keepdims=True)
        hn = hc * lax.rsqrt(var + LN_EPS)
        cols = slice(h * B_V, (h + 1) * B_V)
        hn = hn * nw_ref[:, cols] * jax.nn.sigmoid(op_ref[:, cols])
        h_ref[:, cols] = hn.astype(h_ref.dtype)
    c_out[...] = c_scr[...]
    n_out[...] = n_scr[...]
    m_out[...] = m_scr[...]


def _mlstm_prompt(q, k, v, o_pre, gates, norm_w):
    row = lambda n: pl.BlockSpec((CHUNK, n), lambda c: (c, 0))
    out = [jax.ShapeDtypeStruct((SEQ, HV), BF16),
           jax.ShapeDtypeStruct((B_HEADS, B_QK, B_V), F32),
           jax.ShapeDtypeStruct((8, LANES), F32),
           jax.ShapeDtypeStruct((8, LANES), F32)]
    return pl.pallas_call(
        _mlstm_kernel,
        out_shape=out,
        grid=(SEQ // CHUNK,),
        in_specs=[row(HK), row(HK), row(HV), row(HV), row(LANES),
                  pl.BlockSpec((1, HV), lambda c: (0, 0))],
        out_specs=[row(HV),
                   pl.BlockSpec((B_HEADS, B_QK, B_V), lambda c: (0, 0, 0)),
                   pl.BlockSpec((8, LANES), lambda c: (0, 0)),
                   pl.BlockSpec((8, LANES), lambda c: (0, 0))],
        scratch_shapes=[pltpu.VMEM((B_HEADS, B_QK, B_V), F32),
                        pltpu.VMEM((8, LANES), F32), pltpu.VMEM((8, LANES), F32)],
        compiler_params=_cparams(("arbitrary",)),
        name="mlstm_prompt",
    )(q, k, v, o_pre, gates, norm_w)


def _mlstm_sample_kernel(x_ref, w_ref, wg_ref, gb_ref, cw_ref, cb_ref, conv_ref,
                         c_ref, n_ref, m_ref, nw_ref,
                         h_ref, conv_out, c_out, n_out, m_out):
    xb = x_ref[...].astype(BF16)
    p = jnp.dot(xb, w_ref[...], preferred_element_type=F32)
    gates = jnp.dot(xb, wg_ref[...], preferred_element_type=F32) + gb_ref[...]
    qk_pre = p[0:1, :2 * HK]
    prev = conv_ref[0]
    cw = cw_ref[...]
    acc = cb_ref[...] + cw[3:4] * qk_pre
    for j in range(CONV_K - 1):
        acc = acc + cw[j:j + 1] * prev[j:j + 1]
    qk = acc * jax.nn.sigmoid(acc)
    conv_out[0] = jnp.concatenate([prev[1:3], qk_pre], axis=0)
    v_all = p[0:1, 2 * HK:2 * HK + HV]
    o_pre = p[0:1, 2 * HK + HV:]
    ones8 = jnp.ones((8, 1), F32)
    for h in range(B_HEADS):
        q = qk[:, h * B_QK:(h + 1) * B_QK]
        k = qk[:, HK + h * B_QK:HK + (h + 1) * B_QK] * (B_QK ** -0.5)
        v = v_all[:, h * B_V:(h + 1) * B_V]
        i_pre = gates[0:1, h:h + 1]
        log_f = jax.nn.log_sigmoid(gates[0:1, B_HEADS + h:B_HEADS + h + 1])
        m_prev = m_ref[0, h:h + 1, 0:1]
        n_prev = n_ref[0, h:h + 1, :]
        c_prev = c_ref[0, h]
        m_t = jnp.maximum(log_f + m_prev, i_pre)
        dgate = jnp.exp(i_pre - m_t)
        inter = jnp.exp(log_f + m_prev - m_t)
        s = jnp.sum(q * k, axis=1, keepdims=True) * dgate
        cq = lax.dot_general(ones8 * q, c_prev, (((1,), (1,)), ((), ())),
                             precision=HIGHEST, preferred_element_type=F32)[0:1]
        num = s * v + inter * cq
        den = s + inter * jnp.sum(n_prev * q, axis=1, keepdims=True)
        hv = num / jnp.maximum(jnp.abs(den), jnp.exp(-m_t))
        v_cols = jnp.broadcast_to(v, (B_QK, B_V)).T
        c_out[0, h] = inter * c_prev + dgate * (v_cols * k)
        n_out[0, h:h + 1, :] = inter * n_prev + dgate * k
        m_out[0, h:h + 1, :] = jnp.broadcast_to(m_t, (1, LANES))
        mu = jnp.mean(hv, axis=1, keepdims=True)
        hc = hv - mu
        var = jnp.mean(hc * hc, axis=1, keepdims=True)
        cols = slice(h * B_V, (h + 1) * B_V)
        hn = hc * lax.rsqrt(var + LN_EPS) * nw_ref[:, cols] * jax.nn.sigmoid(o_pre[:, cols])
        h_ref[0, :, cols] = jnp.broadcast_to(hn, (8, B_V))


def _mlstm_sample(x_s8, w_bf, wg_bf, gb, cw, cb, conv_state, c_state, n_state, m_state, norm_w):
    const = lambda s: pl.BlockSpec(s, lambda b: (0,) * len(s))
    per = lambda s: pl.BlockSpec((1,) + s, lambda b: (b,) + (0,) * len(s))
    out = [jax.ShapeDtypeStruct((N_SAMPLE, 8, HV), F32),
           jax.ShapeDtypeStruct((N_SAMPLE, 3, 2 * HK), F32),
           jax.ShapeDtypeStruct((N_SAMPLE, B_HEADS, B_V, B_QK), F32),
           jax.ShapeDtypeStruct((N_SAMPLE, B_HEADS, B_QK), F32),
           jax.ShapeDtypeStruct((N_SAMPLE, B_HEADS, LANES), F32)]
    return pl.pallas_call(
        _mlstm_sample_kernel,
        out_shape=out,
        grid=(N_SAMPLE,),
        in_specs=[pl.BlockSpec((8, D_MODEL), lambda b: (b, 0)),
                  const((D_MODEL, 3 * D_MODEL)), const((D_MODEL, LANES)), const((1, LANES)),
                  const((CONV_K, 2 * HK)), const((1, 2 * HK)),
                  per((3, 2 * HK)), per((B_HEADS, B_V, B_QK)), per((B_HEADS, B_QK)),
                  per((B_HEADS, LANES)), const((1, HV))],
        out_specs=[per((8, HV)), per((3, 2 * HK)), per((B_HEADS, B_V, B_QK)),
                   per((B_HEADS, B_QK)), per((B_HEADS, LANES))],
        compiler_params=_cparams(("arbitrary",)),
        name="mlstm_sample",
    )(x_s8, w_bf, wg_bf, gb, cw, cb, conv_state, c_state, n_state, m_state, norm_w)


def _rope_tables():
    half = A_HEAD_DIM // 2
    pos = jnp.concatenate([jnp.arange(SEQ, dtype=jnp.int32),
                           jnp.full((TM,), PAST_LEN, jnp.int32)])
    inv = ROPE_THETA ** (-jnp.arange(half, dtype=F32) / half)
    ang = pos.astype(F32)[:, None] * inv[None, :]
    cos, sin = jnp.cos(ang), jnp.sin(ang)
    cos_t = jnp.tile(cos, (1, LANES // half))
    sin_t = jnp.tile(jnp.concatenate([-sin, sin], axis=1), (1, LANES // A_HEAD_DIM))
    return cos_t, sin_t


def kernel(x_prompt, x_sample, cache_kv_w128, cache_kv_w512, cache_kv_w2048, state_C, state_n,
           state_m, state_conv, a_w_in, a_w_out, b_w_in, b_conv_w, b_conv_b, b_gate_b, b_norm_w,
           b_w_out, ln_g, ln_b, moe_w_router, moe_b_router, moe_w_gu, moe_b_gu, moe_w_down,
           moe_b_down):
    pad = jnp.zeros((TM - N_SAMPLE, D_MODEL), F32)
    x_all = jnp.concatenate([x_prompt[0], x_sample[:, 0], pad], axis=0)

    cos_t, sin_t = _rope_tables()
    w_in = a_w_in[0].astype(BF16)
    x_slab = jnp.transpose(x_all.reshape(NP, D_MODEL // LANES, LANES), (1, 0, 2))
    groups = _proj_a(x_slab, w_in, cos_t, sin_t)
    tail = _proj_a_tail(x_all, w_in, cos_t, sin_t)
    o_p = _attn_prompt(groups)
    n_tail = ATT_TILE
    qkv_s = tail[:, n_tail:n_tail + N_SAMPLE].reshape(3, N_SAMPLE, 3, A_HEADS, A_HEAD_DIM)
    qkv_s = jnp.transpose(qkv_s, (1, 0, 2, 3, 4))
    caches = (cache_kv_w128, cache_kv_w512, cache_kv_w2048)
    o_s = _attn_sample(qkv_s, caches).reshape(N_SAMPLE, D_MODEL)
    o_s = jnp.concatenate([o_s, pad], axis=0).astype(BF16)
    x_all = _out_proj_ln(o_p, o_s, x_all, a_w_out[0].astype(BF16), ln_g[0, 0][None], ln_b[0, 0][None])
    x_all = _moe_layer(x_all, moe_w_router[0], moe_b_router[0], moe_w_gu[0], moe_b_gu[0],
                       moe_w_down[0], moe_b_down[0], ln_g[0, 1], ln_b[0, 1])

    kv_p, kv_s = [], []
    for g, (d, w) in enumerate(zip(DILATIONS, WINDOWS)):
        k_nat = tail[1, n_tail - w:n_tail, g * D_MODEL:(g + 1) * D_MODEL]
        v_nat = tail[2, n_tail - w:n_tail, g * D_MODEL:(g + 1) * D_MODEL]
        kv = jnp.stack([k_nat, v_nat], axis=1).reshape(1, 1, w, 2, A_HEADS, A_HEAD_DIM)
        kv_p.append(kv)
        kv_s.append(jnp.stack([qkv_s[:, 1, g], qkv_s[:, 2, g]], axis=1)[None, :, None])

    w_b = b_w_in[0]
    w_main = w_b[:, :3 * D_MODEL].astype(BF16)
    w_gate = jnp.pad(w_b[:, 3 * D_MODEL:], ((0, 0), (0, LANES - 2 * B_HEADS))).astype(BF16)
    gb = jnp.pad(b_gate_b[0].reshape(-1), (0, LANES - 2 * B_HEADS))[None]
    cw, cb, nw = b_conv_w[0], b_conv_b[0][None], b_norm_w[0][None]
    q, k, v, o_pre, gates, conv_tail = _proj_b(x_all, w_main, w_gate, gb, cw, cb)
    h_p, c_p, n_p, m_p = _mlstm_prompt(q, k, v, o_pre, gates, nw)
    x_s = x_all[SEQ:SEQ + N_SAMPLE]
    x_s8 = jnp.pad(x_s[:, None, :], ((0, 0), (0, 7), (0, 0))).reshape(N_SAMPLE * 8, D_MODEL)
    m_in = jnp.broadcast_to(state_m[0][:, :, None], (N_SAMPLE, B_HEADS, LANES))
    h_s, conv_s, c_s, n_s, m_s = _mlstm_sample(x_s8, w_main, w_gate, gb, cw, cb, state_conv[0],
                                               state_C[0], state_n[0], m_in, nw)
    h_s = jnp.concatenate([h_s[:, 0, :], pad], axis=0).astype(BF16)
    x_all = _out_proj_ln(h_p, h_s, x_all, b_w_out[0].astype(BF16), ln_g[1, 0][None], ln_b[1, 0][None])
    x_all = _moe_layer(x_all, moe_w_router[1], moe_b_router[1], moe_w_gu[1], moe_b_gu[1],
                       moe_w_down[1], moe_b_down[1], ln_g[1, 1], ln_b[1, 1])

    y_prompt = x_all[:SEQ][None]
    y_sample = x_all[SEQ:SEQ + N_SAMPLE][:, None]
    c_p_out = jnp.swapaxes(c_p, 1, 2)[None, None]
    n_p_out = n_p[:B_HEADS][None, None]
    m_p_out = m_p[:B_HEADS, 0][None, None]
    conv_p = conv_tail[5:8][None, None]
    return (y_prompt, y_sample, kv_p[0], kv_p[1], kv_p[2], c_p_out, n_p_out, m_p_out, conv_p,
            kv_s[0], kv_s[1], kv_s[2], c_s[None], n_s[None], m_s[:, :, 0][None], conv_s[None])
```

```python
import functools

import jax
import jax.numpy as jnp
from jax import lax
from jax.experimental import pallas as pl
from jax.experimental.pallas import tpu as pltpu

F32 = jnp.float32
BF16 = jnp.bfloat16
HIGHEST = lax.Precision.HIGHEST

D_MODEL = 1024
SEQ = 16384
N_SAMPLE = 32
PAST_LEN = 16384
DEPTH = 2
A_HEADS = 16
A_HEAD_DIM = 64
DILATIONS = (1, 4, 16)
WINDOWS = (128, 512, 2048)
ROPE_THETA = 10000.0
B_HEADS = 4
B_QK = 128
B_V = 256
N_EXPERTS = 32
TOP_K = 4
SWIGLU_LIMIT = 7.0
SWIGLU_ALPHA = 1.702
DN_ALPHA = (2 * DEPTH) ** 0.25
LN_EPS = 1e-5

LANES = 128
TM = 256
NT_P = SEQ // TM
NP = SEQ + TM
NT = NP // TM
ATT_TILE = 2048
QB = 128
N_ATT_TILES = SEQ // ATT_TILE
TAIL_T0 = NT_P - ATT_TILE // TM
EXP_TM = 256
N_ASSIGN = NP * TOP_K
N_EXP_TILES = (N_ASSIGN + N_EXPERTS * (EXP_TM - 1)) // EXP_TM + 1
N_SORTED = N_EXP_TILES * EXP_TM
VMEM_LIMIT = 56 * 1024 * 1024


def _cparams(sem):
    return pltpu.CompilerParams(dimension_semantics=sem, vmem_limit_bytes=VMEM_LIMIT)


def _layer_norm(z, g, b):
    mu = jnp.mean(z, axis=-1, keepdims=True)
    zc = z - mu
    var = jnp.mean(zc * zc, axis=-1, keepdims=True)
    return zc * lax.rsqrt(var + LN_EPS) * g + b


def _rope(res, cos, sin, n_lane_tiles):
    rows = res.shape[0]
    lane = lax.broadcasted_iota(jnp.int32, (rows, LANES), 1)
    first = (lane % A_HEAD_DIM) < (A_HEAD_DIM // 2)
    outs = []
    for c in range(n_lane_tiles):
        seg = res[:, c * LANES:(c + 1) * LANES]
        rot = jnp.where(first, pltpu.roll(seg, LANES - 32, 1), pltpu.roll(seg, 32, 1))
        outs.append(seg * cos + rot * sin)
    return outs


def _class_major(ref, lead, d):
    parts = []
    for r in range(d):
        rows = slice(None) if d == 1 else pl.ds(r, TM // d, stride=d)
        parts.append(ref[rows, :] if lead is None else ref[lead, rows, :])
    return parts[0] if d == 1 else jnp.concatenate(parts, axis=0)


def _proj_a_kernel(x_ref, w_ref, cos_ref, sin_ref, o0_ref, o1_ref, o2_ref):
    j = pl.program_id(0)
    rotated = j < 2
    scale = jnp.where(j == 0, A_HEAD_DIM ** -0.5, 1.0).astype(F32)
    per_group = D_MODEL // LANES
    for g, (d, o_ref) in enumerate(zip(DILATIONS, (o0_ref, o1_ref, o2_ref))):
        xg = jnp.concatenate([_class_major(x_ref, c, d) for c in range(per_group)], axis=1)
        res = jnp.dot(xg.astype(BF16), w_ref[:, g * D_MODEL:(g + 1) * D_MODEL],
                      preferred_element_type=F32)
        cos = jnp.where(rotated, _class_major(cos_ref, None, d) * scale, 1.0)
        sin = jnp.where(rotated, _class_major(sin_ref, None, d) * scale, 0.0)
        out = jnp.concatenate(_rope(res, cos, sin, per_group), axis=1).astype(BF16)
        n = TM // d
        for r in range(d):
            o_ref[0, 0, r] = out[r * n:(r + 1) * n]


def _proj_a(x_slab, w_bf, cos, sin):
    outs = []
    specs = []
    for d in DILATIONS:
        outs.append(jax.ShapeDtypeStruct((3, N_ATT_TILES, d, ATT_TILE // d, D_MODEL), BF16))
        specs.append(pl.BlockSpec((1, 1, d, TM // d, D_MODEL),
                                  lambda j, m: (j, m // (ATT_TILE // TM), 0, m % (ATT_TILE // TM), 0)))
    return pl.pallas_call(
        _proj_a_kernel,
        out_shape=outs,
        grid=(3, NT_P),
        in_specs=[
            pl.BlockSpec((D_MODEL // LANES, TM, LANES), lambda j, m: (0, m, 0)),
            pl.BlockSpec((D_MODEL, 3 * D_MODEL), lambda j, m: (0, j)),
            pl.BlockSpec((TM, LANES), lambda j, m: (m, 0)),
            pl.BlockSpec((TM, LANES), lambda j, m: (m, 0)),
        ],
        out_specs=specs,
        compiler_params=_cparams(("arbitrary", "arbitrary")),
        name="proj_a",
    )(x_slab, w_bf, cos, sin)


def _proj_a_tail_kernel(x_ref, w_ref, cos_ref, sin_ref, o_ref):
    j = pl.program_id(0)
    xb = x_ref[...].astype(BF16)
    res = jnp.dot(xb, w_ref[...], preferred_element_type=F32)
    n_tiles = 3 * D_MODEL // LANES

    @pl.when(j < 2)
    def _():
        scale = jnp.where(j == 0, A_HEAD_DIM ** -0.5, 1.0).astype(F32)
        segs = _rope(res, cos_ref[...], sin_ref[...], n_tiles)
        for c in range(n_tiles):
            o_ref[0, :, c * LANES:(c + 1) * LANES] = segs[c] * scale

    @pl.when(j == 2)
    def _():
        o_ref[0] = res


def _proj_a_tail(x_all, w_bf, cos, sin):
    n = NT - TAIL_T0
    return pl.pallas_call(
        _proj_a_tail_kernel,
        out_shape=jax.ShapeDtypeStruct((3, n * TM, 3 * D_MODEL), F32),
        grid=(3, n),
        in_specs=[
            pl.BlockSpec((TM, D_MODEL), lambda j, m: (m + TAIL_T0, 0)),
            pl.BlockSpec((D_MODEL, 3 * D_MODEL), lambda j, m: (0, j)),
            pl.BlockSpec((TM, LANES), lambda j, m: (m + TAIL_T0, 0)),
            pl.BlockSpec((TM, LANES), lambda j, m: (m + TAIL_T0, 0)),
        ],
        out_specs=pl.BlockSpec((1, TM, 3 * D_MODEL), lambda j, m: (j, m, 0)),
        compiler_params=_cparams(("arbitrary", "arbitrary")),
        name="proj_a_tail",
    )(x_all, w_bf, cos, sin)


def _attn_kernel(*refs):
    in_refs = refs[:15]
    o_ref, oscr, lscr = refs[15:]
    jt = pl.program_id(0)
    lane = lax.broadcasted_iota(jnp.int32, (QB, LANES), 1)
    lo = lane < A_HEAD_DIM
    row = lax.broadcasted_iota(jnp.int32, (QB, 2 * QB), 0)
    col = lax.broadcasted_iota(jnp.int32, (QB, 2 * QB), 1)
    mask_cur = (col >= QB) & (col - QB <= row)
    mask_prev = (col < QB) & (col >= row)
    neg = jnp.float32(-jnp.inf)
    bias_full = jnp.concatenate([jnp.where(mask_cur | mask_prev, 0.0, neg)] * 2, axis=0)
    bias_first = jnp.concatenate([jnp.where(mask_cur, 0.0, neg)] * 2, axis=0)
    n_blocks = ATT_TILE // QB
    zero = jnp.zeros((QB, LANES), BF16)

    for g, d in enumerate(DILATIONS):
        q_ref, kc_ref, kp_ref, vc_ref, vp_ref = in_refs[5 * g:5 * g + 5]
        per_class = n_blocks // d
        for b in range(n_blocks):
            r, c = divmod(b, per_class)
            rows = pl.ds(c * QB, QB)
            q = q_ref[0, 0, r, rows, :]
            kc = kc_ref[0, 0, r, rows, :]
            vc = vc_ref[0, 0, r, rows, :]
            if c == 0:
                prow = pl.ds((per_class - 1) * QB, QB)
                kp = kp_ref[0, 0, r, prow, :]
                vp = vp_ref[0, 0, r, prow, :]
                bias = jnp.where(jt > 0, bias_full, bias_first)
            else:
                prow = pl.ds((c - 1) * QB, QB)
                kp = kc_ref[0, 0, r, prow, :]
                vp = vc_ref[0, 0, r, prow, :]
                bias = bias_full
            lhs = jnp.concatenate([jnp.where(lo, q, zero), jnp.where(lo, zero, q)], axis=0)
            k2 = jnp.concatenate([kp, kc], axis=0)
            v2 = jnp.concatenate([vp, vc], axis=0)
            s = lax.dot_general(lhs, k2, (((1,), (1,)), ((), ())), preferred_element_type=F32)
            s = s + bias
            mx = jnp.max(s, axis=1, keepdims=True)
            p = jnp.exp(s - mx)
            den = jnp.sum(p, axis=1, keepdims=True)
            o2 = jnp.dot(p.astype(BF16), v2, preferred_element_type=F32) / den
            lse = mx + jnp.log(den)
            o = jnp.where(lo, o2[:QB], o2[QB:])
            l = jnp.where(lo, jnp.broadcast_to(lse[:QB], (QB, LANES)),
                          jnp.broadcast_to(lse[QB:], (QB, LANES)))
            if d == 1:
                dst = pl.ds(b * QB, QB)
            else:
                dst = pl.ds(c * QB * d + r, QB, stride=d)
            oscr[g, dst, :] = o
            lscr[g, dst, :] = l

    def merge(i, carry):
        rows = pl.ds(pl.multiple_of(i * QB, QB), QB)
        l0, l1, l2 = lscr[0, rows, :], lscr[1, rows, :], lscr[2, rows, :]
        mx = jnp.maximum(jnp.maximum(l0, l1), l2)
        w0, w1, w2 = jnp.exp(l0 - mx), jnp.exp(l1 - mx), jnp.exp(l2 - mx)
        acc = oscr[0, rows, :] * w0 + oscr[1, rows, :] * w1 + oscr[2, rows, :] * w2
        o_ref[rows, :] = (acc / (w0 + w1 + w2)).astype(o_ref.dtype)
        return carry

    lax.fori_loop(0, n_blocks, merge, 0)


def _attn_prompt(qkv_groups):
    ins, specs = [], []
    for g, d in enumerate(DILATIONS):
        arr = qkv_groups[g]
        blk = (1, 1, d, ATT_TILE // d, LANES)
        cur = lambda s: (lambda j, h: (s, j, 0, 0, h))
        prev = lambda s: (lambda j, h: (s, jnp.maximum(j - 1, 0), 0, 0, h))
        for imap in (cur(0), cur(1), prev(1), cur(2), prev(2)):
            ins.append(arr)
            specs.append(pl.BlockSpec(blk, imap))
    return pl.pallas_call(
        _attn_kernel,
        out_shape=jax.ShapeDtypeStruct((SEQ, D_MODEL), BF16),
        grid=(N_ATT_TILES, D_MODEL // LANES),
        in_specs=specs,
        out_specs=pl.BlockSpec((ATT_TILE, LANES), lambda j, h: (j, h)),
        scratch_shapes=[pltpu.VMEM((3, ATT_TILE, LANES), F32),
                        pltpu.VMEM((3, ATT_TILE, LANES), F32)],
        compiler_params=_cparams(("arbitrary", "arbitrary")),
        name="attn_prompt",
    )(*ins)


SAMPLE_HEADS = 4


def _attn_sample_kernel(qkv_ref, c0_ref, c1_ref, c2_ref, o_ref):
    for h in range(SAMPLE_HEADS):
        outs, lses = [], []
        for g, (d, c_ref) in enumerate(zip(DILATIONS, (c0_ref, c1_ref, c2_ref))):
            q = qkv_ref[0, g, h]
            kn = qkv_ref[1, g, h]
            vn = qkv_ref[2, g, h]
            kt = c_ref[0, h]
            vt = c_ref[1, h]
            pos = lax.broadcasted_iota(jnp.int32, (1, kt.shape[1]), 1)
            s = jnp.sum(kt * q, axis=0, keepdims=True)
            s = jnp.where((pos & (d - 1)) == 0, s, -jnp.inf)
            s_self = jnp.sum(q * kn, axis=0, keepdims=True)
            mx = jnp.maximum(jnp.max(s, axis=1, keepdims=True), s_self)
            p = jnp.exp(s - mx)
            p_self = jnp.exp(s_self - mx)
            den = jnp.sum(p, axis=1, keepdims=True) + p_self
            acc = jnp.sum(vt * p, axis=1, keepdims=True) + p_self * vn
            outs.append(acc / den)
            lses.append(mx + jnp.log(den))
        mx = jnp.maximum(jnp.maximum(lses[0], lses[1]), lses[2])
        ws = [jnp.exp(l - mx) for l in lses]
        acc = outs[0] * ws[0] + outs[1] * ws[1] + outs[2] * ws[2]
        o_ref[h] = acc / (ws[0] + ws[1] + ws[2])


def _attn_sample(qkv_cols, caches_t):
    hg = SAMPLE_HEADS
    specs = [pl.BlockSpec((None, 3, 3, hg, A_HEAD_DIM, 1), lambda b, h: (b, 0, 0, h, 0, 0))]
    for c in caches_t:
        specs.append(pl.BlockSpec((None, 2, hg, A_HEAD_DIM, c.shape[-1]),
                                  lambda b, h: (b, 0, h, 0, 0)))
    return pl.pallas_call(
        _attn_sample_kernel,
        out_shape=jax.ShapeDtypeStruct((N_SAMPLE, A_HEADS, A_HEAD_DIM, 1), F32),
        grid=(N_SAMPLE, A_HEADS // hg),
        in_specs=specs,
        out_specs=pl.BlockSpec((None, hg, A_HEAD_DIM, 1), lambda b, h: (b, h, 0, 0)),
        compiler_params=_cparams(("arbitrary", "arbitrary")),
        name="attn_sample",
    )(qkv_cols, *caches_t)


def _out_proj_ln_kernel(yp_ref, ys_ref, x_ref, w_ref, g_ref, b_ref, o_ref):
    m = pl.program_id(0)
    y = jnp.where(m < NT_P, yp_ref[...], ys_ref[...])
    proj = jnp.dot(y, w_ref[...], preferred_element_type=F32)
    o_ref[...] = _layer_norm(DN_ALPHA * x_ref[...] + proj, g_ref[...], b_ref[...])


def _out_proj_ln(y_p, y_s, x_all, w_bf, g, b):
    return pl.pallas_call(
        _out_proj_ln_kernel,
        out_shape=jax.ShapeDtypeStruct((NP, D_MODEL), F32),
        grid=(NT,),
        in_specs=[
            pl.BlockSpec((TM, D_MODEL), lambda m: (jnp.minimum(m, NT_P - 1), 0)),
            pl.BlockSpec((TM, D_MODEL), lambda m: (0, 0)),
            pl.BlockSpec((TM, D_MODEL), lambda m: (m, 0)),
            pl.BlockSpec((D_MODEL, D_MODEL), lambda m: (0, 0)),
            pl.BlockSpec((1, D_MODEL), lambda m: (0, 0)),
            pl.BlockSpec((1, D_MODEL), lambda m: (0, 0)),
        ],
        out_specs=pl.BlockSpec((TM, D_MODEL), lambda m: (m, 0)),
        compiler_params=_cparams(("arbitrary",)),
        name="out_proj_ln",
    )(y_p, y_s, x_all, w_bf, g, b)


def _router_kernel(x_ref, w_ref, b_ref, route_ref, gate_ref, cnt_ref, run_scr):
    m = pl.program_id(0)

    @pl.when(m == 0)
    def _():
        run_scr[...] = jnp.zeros_like(run_scr)

    logits = jnp.dot(x_ref[...], w_ref[...], precision=HIGHEST,
                     preferred_element_type=F32) + b_ref[...]
    lane = lax.broadcasted_iota(jnp.int32, (TM, LANES), 1)
    tri = (lax.broadcasted_iota(jnp.int32, (TM, TM), 1)
           < lax.broadcasted_iota(jnp.int32, (TM, TM), 0)).astype(BF16)
    running = run_scr[0:1, :]
    packed = jnp.zeros((TM, LANES), F32)
    vals = jnp.zeros((TM, LANES), F32)
    work = logits
    for k in range(TOP_K):
        best = jnp.max(work, axis=1, keepdims=True)
        idx = jnp.min(jnp.where(work == best, lane, LANES), axis=1, keepdims=True)
        sel = lane == idx
        work = jnp.where(sel, -jnp.inf, work)
        onehot = sel.astype(BF16)
        before = jnp.dot(tri, onehot, preferred_element_type=F32) + running
        rank = jnp.sum(jnp.where(sel, before, 0.0), axis=1, keepdims=True)
        running = running + jnp.sum(onehot.astype(F32), axis=0, keepdims=True)
        packed = jnp.where(lane == k, idx.astype(F32), packed)
        packed = jnp.where(lane == TOP_K + k, rank, packed)
        vals = jnp.where(lane == k, best, vals)
    top = lane < TOP_K
    e = jnp.where(top, jnp.exp(vals - vals[:, 0:1]), 0.0)
    gate_ref[...] = e / jnp.sum(e, axis=1, keepdims=True)
    route_ref[...] = packed.T[:2 * TOP_K, :].astype(jnp.int32)
    run_scr[...] = jnp.broadcast_to(running, run_scr.shape)
    cnt_ref[...] = jnp.broadcast_to(running, cnt_ref.shape)


def _router(x_all, w_pad, b_pad):
    out = [jax.ShapeDtypeStruct((2 * TOP_K, NP), jnp.int32),
           jax.ShapeDtypeStruct((NP, LANES), F32),
           jax.ShapeDtypeStruct((8, LANES), F32)]
    return pl.pallas_call(
        _router_kernel,
        out_shape=out,
        grid=(NT,),
        in_specs=[pl.BlockSpec((TM, D_MODEL), lambda m: (m, 0)),
                  pl.BlockSpec((D_MODEL, LANES), lambda m: (0, 0)),
                  pl.BlockSpec((1, LANES), lambda m: (0, 0))],
        out_specs=[pl.BlockSpec((2 * TOP_K, TM), lambda m: (0, m)),
                   pl.BlockSpec((TM, LANES), lambda m: (m, 0)),
                   pl.BlockSpec((8, LANES), lambda m: (0, 0))],
        scratch_shapes=[pltpu.VMEM((8, LANES), F32)],
        compiler_params=_cparams(("arbitrary",)),
        name="router",
    )(x_all, w_pad, b_pad)


def _dispatch_kernel(start_ref, ltile_ref, nv_ref, route_ref, x_ref, xs_ref, pos_ref,
                     zbuf, sem, zsem):
    m = pl.program_id(0)

    @pl.when(m == 0)
    def _():
        zbuf[...] = jnp.zeros_like(zbuf)

        def zero_copy(tile):
            row = pl.multiple_of(tile * EXP_TM, EXP_TM)
            return pltpu.make_async_copy(zbuf, xs_ref.at[pl.ds(row, EXP_TM), :], zsem)

        for e in range(N_EXPERTS):
            @pl.when(ltile_ref[e] >= 0)
            def _():
                zero_copy(jnp.maximum(ltile_ref[e], 0)).start()

        def start_tail(i, carry):
            zero_copy(i).start()
            return carry

        def wait_tail(i, carry):
            zero_copy(i).wait()
            return carry

        lax.fori_loop(nv_ref[0], N_EXP_TILES, start_tail, 0)
        for e in range(N_EXPERTS):
            @pl.when(ltile_ref[e] >= 0)
            def _():
                zero_copy(jnp.maximum(ltile_ref[e], 0)).wait()
        lax.fori_loop(nv_ref[0], N_EXP_TILES, wait_tail, 0)

    def issue(t, carry):
        for k in range(TOP_K):
            pos = start_ref[route_ref[k, t]] + route_ref[TOP_K + k, t]
            pos_ref[t * TOP_K + k] = pos
            pltpu.make_async_copy(x_ref.at[pl.ds(t, 1), :], xs_ref.at[pl.ds(pos, 1), :],
                                  sem).start(priority=k % 2)
        return carry

    lax.fori_loop(0, TM, issue, 0, unroll=8)
    for k in range(TOP_K):
        pltpu.make_async_copy(x_ref, xs_ref.at[pl.ds(0, TM), :], sem).wait()


def _dispatch(start, last_tile, n_valid, route, x_all):
    grid_spec = pltpu.PrefetchScalarGridSpec(
        num_scalar_prefetch=3,
        grid=(NT,),
        in_specs=[pl.BlockSpec((2 * TOP_K, TM), lambda m, *_: (0, m), memory_space=pltpu.SMEM),
                  pl.BlockSpec((TM, D_MODEL), lambda m, *_: (m, 0))],
        out_specs=[pl.BlockSpec(memory_space=pl.ANY),
                   pl.BlockSpec((TM * TOP_K,), lambda m, *_: (m,), memory_space=pltpu.SMEM)],
        scratch_shapes=[pltpu.VMEM((EXP_TM, D_MODEL), F32),
                        pltpu.SemaphoreType.DMA, pltpu.SemaphoreType.DMA],
    )
    return pl.pallas_call(
        _dispatch_kernel,
        out_shape=[jax.ShapeDtypeStruct((N_SORTED, D_MODEL), F32),
                   jax.ShapeDtypeStruct((NP * TOP_K,), jnp.int32)],
        grid_spec=grid_spec,
        compiler_params=_cparams(("arbitrary",)),
        name="moe_dispatch",
    )(start, last_tile, n_valid, route, x_all)


def _experts_kernel(te_ref, nv_ref, x_ref, wgu_ref, bgu_ref, wdn_ref, bdn_ref, o_ref,
                    wgu_bf, wdn_bf):
    i = pl.program_id(0)
    prev = te_ref[jnp.maximum(i - 1, 0)]
    fresh = (i == 0) | (te_ref[i] != prev)

    @pl.when(fresh)
    def _():
        wgu_bf[...] = wgu_ref[0].astype(BF16)
        wdn_bf[...] = wdn_ref[0].astype(BF16)

    @pl.when(i < nv_ref[0])
    def _():
        xb = x_ref[...].astype(BF16)
        h = jnp.dot(xb, wgu_bf[...], preferred_element_type=F32) + bgu_ref[0]
        hg = jnp.minimum(h[:, :D_MODEL], SWIGLU_LIMIT)
        hu = jnp.clip(h[:, D_MODEL:], -SWIGLU_LIMIT, SWIGLU_LIMIT)
        act = hg * jax.nn.sigmoid(SWIGLU_ALPHA * hg) * (hu + 1.0)
        o_ref[...] = jnp.dot(act.astype(BF16), wdn_bf[...],
                             preferred_element_type=F32) + bdn_ref[0]

    @pl.when(i >= nv_ref[0])
    def _():
        o_ref[...] = jnp.zeros_like(o_ref)


def _experts(layer, tile_expert, n_valid, x_sorted, w_gu, b_gu, w_dn, b_dn):
    def row_map(i, te, nv):
        return (jnp.maximum(jnp.minimum(i, nv[0] - 1), 0), 0)

    def expert_map(i, te, nv):
        return (layer, te[i], 0, 0)

    grid_spec = pltpu.PrefetchScalarGridSpec(
        num_scalar_prefetch=2,
        grid=(N_EXP_TILES,),
        in_specs=[
            pl.BlockSpec((EXP_TM, D_MODEL), row_map),
            pl.BlockSpec((None, 1, D_MODEL, 2 * D_MODEL), expert_map),
            pl.BlockSpec((None, 1, 1, 2 * D_MODEL), expert_map),
            pl.BlockSpec((None, 1, D_MODEL, D_MODEL), expert_map),
            pl.BlockSpec((None, 1, 1, D_MODEL), expert_map),
        ],
        out_specs=pl.BlockSpec((EXP_TM, D_MODEL), lambda i, te, nv: (i, 0)),
        scratch_shapes=[pltpu.VMEM((D_MODEL, 2 * D_MODEL), BF16),
                        pltpu.VMEM((D_MODEL, D_MODEL), BF16)],
    )
    return pl.pallas_call(
        _experts_kernel,
        out_shape=jax.ShapeDtypeStruct((N_SORTED, D_MODEL), F32),
        grid_spec=grid_spec,
        compiler_params=_cparams(("arbitrary",)),
        name="moe_experts",
    )(tile_expert, n_valid, x_sorted, w_gu, b_gu, w_dn, b_dn)


CMB_TM = 128
N_CMB = NP // CMB_TM


def _combine_ln_kernel(pos_ref, nxt_ref, gate_ref, x_ref, g_ref, b_ref, ys_ref, o_ref, buf, sem):
    m = pl.program_id(0)

    def fetch(idx_ref, slot):
        def issue(t, carry):
            for k in range(TOP_K):
                pltpu.make_async_copy(ys_ref.at[pl.ds(idx_ref[t * TOP_K + k], 1), :],
                                      buf.at[slot, k, pl.ds(t, 1), :],
                                      sem.at[slot]).start(priority=k % 2)
            return carry

        lax.fori_loop(0, CMB_TM, issue, 0, unroll=8)

    @pl.when(m == 0)
    def _():
        fetch(pos_ref, 0)

    @pl.when(m + 1 < N_CMB)
    def _():
        fetch(nxt_ref, (m + 1) % 2)

    slot = m % 2
    pltpu.make_async_copy(buf.at[slot], buf.at[slot], sem.at[slot]).wait()
    gate = gate_ref[...]
    acc = buf[slot, 0] * gate[:, 0:1]
    for k in range(1, TOP_K):
        acc = acc + buf[slot, k] * gate[:, k:k + 1]
    o_ref[...] = _layer_norm(DN_ALPHA * x_ref[...] + acc, g_ref[...], b_ref[...])


def _combine_ln(pos_flat, gate, x_all, g, b, y_sorted):
    idx = lambda off: pl.BlockSpec((CMB_TM * TOP_K,),
                                   lambda m: (jnp.minimum(m + off, N_CMB - 1),),
                                   memory_space=pltpu.SMEM)
    return pl.pallas_call(
        _combine_ln_kernel,
        out_shape=jax.ShapeDtypeStruct((NP, D_MODEL), F32),
        grid=(N_CMB,),
        in_specs=[idx(0), idx(1),
                  pl.BlockSpec((CMB_TM, LANES), lambda m: (m, 0)),
                  pl.BlockSpec((CMB_TM, D_MODEL), lambda m: (m, 0)),
                  pl.BlockSpec((1, D_MODEL), lambda m: (0, 0)),
                  pl.BlockSpec((1, D_MODEL), lambda m: (0, 0)),
                  pl.BlockSpec(memory_space=pl.ANY)],
        out_specs=pl.BlockSpec((CMB_TM, D_MODEL), lambda m: (m, 0)),
        scratch_shapes=[pltpu.VMEM((2, TOP_K, CMB_TM, D_MODEL), F32),
                        pltpu.SemaphoreType.DMA((2,))],
        compiler_params=_cparams(("arbitrary",)),
        name="moe_combine_ln",
    )(pos_flat, pos_flat, gate, x_all, g, b, y_sorted)


def _moe_layer(layer, x_all, w_r, b_r, w_gu, b_gu, w_dn, b_dn, g, b):
    w_pad = jnp.pad(w_r, ((0, 0), (0, LANES - N_EXPERTS)))
    b_pad = jnp.pad(b_r, (0, LANES - N_EXPERTS), constant_values=-jnp.inf)[None]
    route, gate, counts = _router(x_all, w_pad, b_pad)
    counts = counts[0, :N_EXPERTS].astype(jnp.int32)
    tiles = (counts + EXP_TM - 1) // EXP_TM
    tile_end = jnp.cumsum(tiles)
    start = ((tile_end - tiles) * EXP_TM).astype(jnp.int32)
    last_tile = jnp.where(tiles > 0, tile_end - 1, -1).astype(jnp.int32)
    n_valid = tile_end[-1:].astype(jnp.int32)
    tile_ids = jnp.arange(N_EXP_TILES, dtype=jnp.int32)
    tile_expert = jnp.minimum(
        jnp.sum((tile_ids[:, None] >= tile_end[None, :]).astype(jnp.int32), axis=1),
        N_EXPERTS - 1).astype(jnp.int32)
    tile_expert = jnp.where(tile_ids < n_valid[0], tile_expert,
                            jnp.take(tile_expert, jnp.maximum(n_valid[0] - 1, 0)))
    x_sorted, pos_flat = _dispatch(start, last_tile, n_valid, route, x_all)
    y_sorted = _experts(layer, tile_expert, n_valid, x_sorted, w_gu, b_gu, w_dn, b_dn)
    return _combine_ln(pos_flat, gate, x_all, g[None], b[None], y_sorted)


CONV_K = 4
HK = B_HEADS * B_QK
HV = B_HEADS * B_V


def _conv_silu(prev3, cur, cw, cb):
    rows = cur.shape[0]
    ext = jnp.concatenate([jnp.zeros((5, 2 * HK), F32), prev3, cur], axis=0)
    acc = cb + cw[CONV_K - 1:CONV_K] * cur
    for j in range(CONV_K - 1):
        acc = acc + cw[j:j + 1] * ext[5 + j:5 + j + rows]
    return acc * jax.nn.sigmoid(acc)


def _proj_b_kernel(x_ref, w_ref, wg_ref, gb_ref, cw_ref, cb_ref,
                   q_ref, k_ref, v_ref, op_ref, gt_ref, conv_ref, carry):
    m = pl.program_id(0)

    @pl.when(m == 0)
    def _():
        carry[...] = jnp.zeros_like(carry)

    xb = x_ref[...].astype(BF16)
    p = jnp.dot(xb, w_ref[...], preferred_element_type=F32)
    gates = jnp.dot(xb, wg_ref[...], preferred_element_type=F32) + gb_ref[...]
    lane = lax.broadcasted_iota(jnp.int32, gates.shape, 1)
    is_f = (lane >= B_HEADS) & (lane < 2 * B_HEADS)
    gt_ref[...] = jnp.where(is_f, jax.nn.log_sigmoid(gates), gates)
    qk_pre = p[:, :2 * HK]
    qk = _conv_silu(carry[5:8, :], qk_pre, cw_ref[...], cb_ref[...])
    q_ref[...] = qk[:, :HK].astype(BF16)
    k_ref[...] = (qk[:, HK:] * (B_QK ** -0.5)).astype(BF16)
    v_ref[...] = p[:, 2 * HK:2 * HK + HV].astype(BF16)
    op_ref[...] = p[:, 2 * HK + HV:]
    carry[...] = qk_pre[TM - 8:, :]
    conv_ref[...] = qk_pre[TM - 8:, :]


def _proj_b(x_all, w_bf, wg_bf, gb, cw, cb):
    row = lambda n: pl.BlockSpec((TM, n), lambda m: (m, 0))
    const = lambda s: pl.BlockSpec(s, lambda m: (0, 0))
    out = [jax.ShapeDtypeStruct((SEQ, HK), BF16), jax.ShapeDtypeStruct((SEQ, HK), BF16),
           jax.ShapeDtypeStruct((SEQ, HV), BF16), jax.ShapeDtypeStruct((SEQ, HV), F32),
           jax.ShapeDtypeStruct((SEQ, LANES), F32), jax.ShapeDtypeStruct((8, 2 * HK), F32)]
    return pl.pallas_call(
        _proj_b_kernel,
        out_shape=out,
        grid=(NT_P,),
        in_specs=[row(D_MODEL), const((D_MODEL, 3 * D_MODEL)), const((D_MODEL, LANES)),
                  const((1, LANES)), const((CONV_K, 2 * HK)), const((1, 2 * HK))],
        out_specs=[row(HK), row(HK), row(HV), row(HV), row(LANES), const((8, 2 * HK))],
        scratch_shapes=[pltpu.VMEM((8, 2 * HK), F32)],
        compiler_params=_cparams(("arbitrary",)),
        name="proj_b",
    )(x_all, w_bf, wg_bf, gb, cw, cb)


CHUNK = 128


def _mlstm_kernel(q_ref, k_ref, v_ref, op_ref, gt_ref, nw_ref,
                  h_ref, c_out, n_out, m_out, c_scr, n_scr, m_scr):
    ci = pl.program_id(0)

    @pl.when(ci == 0)
    def _():
        c_scr[...] = jnp.zeros_like(c_scr)
        n_scr[...] = jnp.zeros_like(n_scr)
        m_scr[...] = jnp.zeros_like(m_scr)

    gates = gt_ref[...]
    t_idx = lax.broadcasted_iota(jnp.int32, (CHUNK, CHUNK), 0)
    s_idx = lax.broadcasted_iota(jnp.int32, (CHUNK, CHUNK), 1)
    causal = s_idx <= t_idx
    tri = causal.astype(F32)
    b_cols = jnp.dot(tri, gates, precision=HIGHEST, preferred_element_type=F32)
    gates_t = gates.T
    b_rows = jnp.dot(gates_t, (t_idx <= s_idx).astype(F32), precision=HIGHEST,
                     preferred_element_type=F32)
    for h in range(B_HEADS):
        q = q_ref[:, h * B_QK:(h + 1) * B_QK]
        k = k_ref[:, h * B_QK:(h + 1) * B_QK]
        v = v_ref[:, h * B_V:(h + 1) * B_V]
        m_prev = m_scr[h:h + 1, 0:1]
        n_prev = n_scr[h:h + 1, :]
        c_prev = c_scr[h]
        b_col = b_cols[:, B_HEADS + h:B_HEADS + h + 1]
        i_col = gates[:, h:h + 1]
        b_row = b_rows[B_HEADS + h:B_HEADS + h + 1, :]
        i_row = gates_t[h:h + 1, :]
        log_d = jnp.where(causal, b_col + (i_row - b_row), -jnp.inf)
        m_t = jnp.maximum(b_col + m_prev, jnp.max(log_d, axis=1, keepdims=True))
        dmat = jnp.exp(log_d - m_t)
        inter = jnp.exp(b_col + m_prev - m_t)
        s = lax.dot_general(q, k, (((1,), (1,)), ((), ())), preferred_element_type=F32) * dmat
        qf = q.astype(F32)
        num = (jnp.dot(s.astype(BF16), v, preferred_element_type=F32)
               + inter * jnp.dot(q, c_prev.astype(BF16), preferred_element_type=F32))
        den = (jnp.sum(s, axis=1, keepdims=True)
               + inter * jnp.sum(qf * n_prev, axis=1, keepdims=True))
        hv = num / jnp.maximum(jnp.abs(den), jnp.exp(-m_t))
        m_new = m_t[CHUNK - 1:CHUNK, :]
        b_last = b_col[CHUNK - 1:CHUNK, :]
        decay = jnp.exp(b_last - b_col + i_col - m_new)
        gfac = jnp.exp(b_last + m_prev - m_new)
        kd = k.astype(F32) * decay
        c_scr[h] = gfac * c_prev + jnp.dot(kd.T.astype(BF16), v, preferred_element_type=F32)
        n_scr[h:h + 1, :] = gfac * n_prev + jnp.sum(kd, axis=0, keepdims=True)
        m_scr[h:h + 1, :] = jnp.broadcast_to(m_new, (1, LANES))
        mu = jnp.mean(hv, axis=1, keepdims=True)
        hc = hv - mu
        var = jnp.mean(hc * hc, axis=1, keepdims=True)
        hn = hc * lax.rsqrt(var + LN_EPS)
        cols = slice(h * B_V, (h + 1) * B_V)
        hn = hn * nw_ref[:, cols] * jax.nn.sigmoid(op_ref[:, cols])
        h_ref[:, cols] = hn.astype(h_ref.dtype)
    c_out[...] = c_scr[...]
    n_out[...] = n_scr[...]
    m_out[...] = m_scr[...]


def _mlstm_prompt(q, k, v, o_pre, gates, norm_w):
    row = lambda n: pl.BlockSpec((CHUNK, n), lambda c: (c, 0))
    out = [jax.ShapeDtypeStruct((SEQ, HV), BF16),
           jax.ShapeDtypeStruct((B_HEADS, B_QK, B_V), F32),
           jax.ShapeDtypeStruct((8, LANES), F32),
           jax.ShapeDtypeStruct((8, LANES), F32)]
    return pl.pallas_call(
        _mlstm_kernel,
        out_shape=out,
        grid=(SEQ // CHUNK,),
        in_specs=[row(HK), row(HK), row(HV), row(HV), row(LANES),
                  pl.BlockSpec((1, HV), lambda c: (0, 0))],
        out_specs=[row(HV),
                   pl.BlockSpec((B_HEADS, B_QK, B_V), lambda c: (0, 0, 0)),
                   pl.BlockSpec((8, LANES), lambda c: (0, 0)),
                   pl.BlockSpec((8, LANES), lambda c: (0, 0))],
        scratch_shapes=[pltpu.VMEM((B_HEADS, B_QK, B_V), F32),
                        pltpu.VMEM((8, LANES), F32), pltpu.VMEM((8, LANES), F32)],
        compiler_params=_cparams(("arbitrary",)),
        name="mlstm_prompt",
    )(q, k, v, o_pre, gates, norm_w)


def _mlstm_sample_kernel(x_ref, w_ref, wg_ref, gb_ref, cw_ref, cb_ref, conv_ref,
                         c_ref, n_ref, m_ref, nw_ref,
                         h_ref, conv_out, c_out, n_out, m_out):
    xb = x_ref[...].astype(BF16)
    p = jnp.dot(xb, w_ref[...], preferred_element_type=F32)
    gates = jnp.dot(xb, wg_ref[...], preferred_element_type=F32) + gb_ref[...]
    qk_pre = p[0:1, :2 * HK]
    prev = conv_ref[0]
    cw = cw_ref[...]
    acc = cb_ref[...] + cw[3:4] * qk_pre
    for j in range(CONV_K - 1):
        acc = acc + cw[j:j + 1] * prev[j:j + 1]
    qk = acc * jax.nn.sigmoid(acc)
    conv_out[0] = jnp.concatenate([prev[1:3], qk_pre], axis=0)
    v_all = p[0:1, 2 * HK:2 * HK + HV]
    o_pre = p[0:1, 2 * HK + HV:]
    ones8 = jnp.ones((8, 1), F32)
    for h in range(B_HEADS):
        q = qk[:, h * B_QK:(h + 1) * B_QK]
        k = qk[:, HK + h * B_QK:HK + (h + 1) * B_QK] * (B_QK ** -0.5)
        v = v_all[:, h * B_V:(h + 1) * B_V]
        i_pre = gates[0:1, h:h + 1]
        log_f = jax.nn.log_sigmoid(gates[0:1, B_HEADS + h:B_HEADS + h + 1])
        m_prev = m_ref[0, h:h + 1, 0:1]
        n_prev = n_ref[0, h:h + 1, :]
        c_prev = c_ref[0, h]
        m_t = jnp.maximum(log_f + m_prev, i_pre)
        dgate = jnp.exp(i_pre - m_t)
        inter = jnp.exp(log_f + m_prev - m_t)
        s = jnp.sum(q * k, axis=1, keepdims=True) * dgate
        cq = lax.dot_general(ones8 * q, c_prev, (((1,), (1,)), ((), ())),
                             precision=HIGHEST, preferred_element_type=F32)[0:1]
        num = s * v + inter * cq
        den = s + inter * jnp.sum(n_prev * q, axis=1, keepdims=True)
        hv = num / jnp.maximum(jnp.abs(den), jnp.exp(-m_t))
        v_cols = jnp.broadcast_to(v, (B_QK, B_V)).T
        c_out[0, h] = inter * c_prev + dgate * (v_cols * k)
        n_out[0, h:h + 1, :] = inter * n_prev + dgate * k
        m_out[0, h:h + 1, :] = jnp.broadcast_to(m_t, (1, LANES))
        mu = jnp.mean(hv, axis=1, keepdims=True)
        hc = hv - mu
        var = jnp.mean(hc * hc, axis=1, keepdims=True)
        cols = slice(h * B_V, (h + 1) * B_V)
        hn = hc * lax.rsqrt(var + LN_EPS) * nw_ref[:, cols] * jax.nn.sigmoid(o_pre[:, cols])
        h_ref[0, :, cols] = jnp.broadcast_to(hn, (8, B_V))


def _mlstm_sample(x_s8, w_bf, wg_bf, gb, cw, cb, conv_state, c_state, n_state, m_state, norm_w):
    const = lambda s: pl.BlockSpec(s, lambda b: (0,) * len(s))
    per = lambda s: pl.BlockSpec((1,) + s, lambda b: (b,) + (0,) * len(s))
    out = [jax.ShapeDtypeStruct((N_SAMPLE, 8, HV), F32),
           jax.ShapeDtypeStruct((N_SAMPLE, 3, 2 * HK), F32),
           jax.ShapeDtypeStruct((N_SAMPLE, B_HEADS, B_V, B_QK), F32),
           jax.ShapeDtypeStruct((N_SAMPLE, B_HEADS, B_QK), F32),
           jax.ShapeDtypeStruct((N_SAMPLE, B_HEADS, LANES), F32)]
    return pl.pallas_call(
        _mlstm_sample_kernel,
        out_shape=out,
        grid=(N_SAMPLE,),
        in_specs=[pl.BlockSpec((8, D_MODEL), lambda b: (b, 0)),
                  const((D_MODEL, 3 * D_MODEL)), const((D_MODEL, LANES)), const((1, LANES)),
                  const((CONV_K, 2 * HK)), const((1, 2 * HK)),
                  per((3, 2 * HK)), per((B_HEADS, B_V, B_QK)), per((B_HEADS, B_QK)),
                  per((B_HEADS, LANES)), const((1, HV))],
        out_specs=[per((8, HV)), per((3, 2 * HK)), per((B_HEADS, B_V, B_QK)),
                   per((B_HEADS, B_QK)), per((B_HEADS, LANES))],
        compiler_params=_cparams(("arbitrary",)),
        name="mlstm_sample",
    )(x_s8, w_bf, wg_bf, gb, cw, cb, conv_state, c_state, n_state, m_state, norm_w)


def _rope_tables():
    half = A_HEAD_DIM // 2
    pos = jnp.concatenate([jnp.arange(SEQ, dtype=jnp.int32),
                           jnp.full((TM,), PAST_LEN, jnp.int32)])
    inv = ROPE_THETA ** (-jnp.arange(half, dtype=F32) / half)
    ang = pos.astype(F32)[:, None] * inv[None, :]
    cos, sin = jnp.cos(ang), jnp.sin(ang)
    cos_t = jnp.tile(cos, (1, LANES // half))
    sin_t = jnp.tile(jnp.concatenate([-sin, sin], axis=1), (1, LANES // A_HEAD_DIM))
    return cos_t, sin_t


def kernel(x_prompt, x_sample, cache_kv_w128, cache_kv_w512, cache_kv_w2048, state_C, state_n,
           state_m, state_conv, a_w_in, a_w_out, b_w_in, b_conv_w, b_conv_b, b_gate_b, b_norm_w,
           b_w_out, ln_g, ln_b, moe_w_router, moe_b_router, moe_w_gu, moe_b_gu, moe_w_down,
           moe_b_down):
    pad = jnp.zeros((TM - N_SAMPLE, D_MODEL), F32)
    x_all = jnp.concatenate([x_prompt[0], x_sample[:, 0], pad], axis=0)

    cos_t, sin_t = _rope_tables()
    w_in = a_w_in[0].astype(BF16)
    x_slab = jnp.transpose(x_all.reshape(NP, D_MODEL // LANES, LANES), (1, 0, 2))
    groups = _proj_a(x_slab, w_in, cos_t, sin_t)
    tail = _proj_a_tail(x_all, w_in, cos_t, sin_t)
    o_p = _attn_prompt(groups)
    n_tail = ATT_TILE
    qkv_s = tail[:, n_tail:n_tail + N_SAMPLE].reshape(3, N_SAMPLE, 3, A_HEADS, A_HEAD_DIM)
    qkv_s = jnp.transpose(qkv_s, (1, 0, 2, 3, 4))
    caches_t = [jnp.transpose(c[0], (0, 2, 3, 4, 1))
                for c in (cache_kv_w128, cache_kv_w512, cache_kv_w2048)]
    o_s = _attn_sample(qkv_s[..., None], caches_t).reshape(N_SAMPLE, D_MODEL)
    o_s = jnp.concatenate([o_s, pad], axis=0).astype(BF16)
    x_all = _out_proj_ln(o_p, o_s, x_all, a_w_out[0].astype(BF16), ln_g[0, 0][None], ln_b[0, 0][None])
    b_gu4, b_dn4 = moe_b_gu[:, :, None, :], moe_b_down[:, :, None, :]
    x_all = _moe_layer(0, x_all, moe_w_router[0], moe_b_router[0], moe_w_gu, b_gu4,
                       moe_w_down, b_dn4, ln_g[0, 1], ln_b[0, 1])

    kv_p, kv_s = [], []
    for g, (d, w) in enumerate(zip(DILATIONS, WINDOWS)):
        k_nat = tail[1, n_tail - w:n_tail, g * D_MODEL:(g + 1) * D_MODEL]
        v_nat = tail[2, n_tail - w:n_tail, g * D_MODEL:(g + 1) * D_MODEL]
        kv = jnp.stack([k_nat, v_nat], axis=1).reshape(1, 1, w, 2, A_HEADS, A_HEAD_DIM)
        kv_p.append(kv)
        kv_s.append(jnp.stack([qkv_s[:, 1, g], qkv_s[:, 2, g]], axis=1)[None, :, None])

    w_b = b_w_in[0]
    w_main = w_b[:, :3 * D_MODEL].astype(BF16)
    w_gate = jnp.pad(w_b[:, 3 * D_MODEL:], ((0, 0), (0, LANES - 2 * B_HEADS))).astype(BF16)
    gb = jnp.pad(b_gate_b[0].reshape(-1), (0, LANES - 2 * B_HEADS))[None]
    cw, cb, nw = b_conv_w[0], b_conv_b[0][None], b_norm_w[0][None]
    q, k, v, o_pre, gates, conv_tail = _proj_b(x_all, w_main, w_gate, gb, cw, cb)
    h_p, c_p, n_p, m_p = _mlstm_prompt(q, k, v, o_pre, gates, nw)
    x_s = x_all[SEQ:SEQ + N_SAMPLE]
    x_s8 = jnp.pad(x_s[:, None, :], ((0, 0), (0, 7), (0, 0))).reshape(N_SAMPLE * 8, D_MODEL)
    m_in = jnp.broadcast_to(state_m[0][:, :, None], (N_SAMPLE, B_HEADS, LANES))
    h_s, conv_s, c_s, n_s, m_s = _mlstm_sample(x_s8, w_main, w_gate, gb, cw, cb, state_conv[0],
                                               state_C[0], state_n[0], m_in, nw)
    h_s = jnp.concatenate([h_s[:, 0, :], pad], axis=0).astype(BF16)
    x_all = _out_proj_ln(h_p, h_s, x_all, b_w_out[0].astype(BF16), ln_g[1, 0][None], ln_b[1, 0][None])
    x_all = _moe_layer(1, x_all, moe_w_router[1], moe_b_router[1], moe_w_gu, b_gu4,
                       moe_w_down, b_dn4, ln_g[1, 1], ln_b[1, 1])

    y_prompt = x_all[:SEQ][None]
    y_sample = x_all[SEQ:SEQ + N_SAMPLE][:, None]
    c_p_out = jnp.swapaxes(c_p, 1, 2)[None, None]
    n_p_out = n_p[:B_HEADS][None, None]
    m_p_out = m_p[:B_HEADS, 0][None, None]
    conv_p = conv_tail[5:8][None, None]
    return (y_prompt, y_sample, kv_p[0], kv_p[1], kv_p[2], c_p_out, n_p_out, m_p_out, conv_p,
            kv_s[0], kv_s[1], kv_s[2], c_s[None], n_s[None], m_s[:, :, 0][None], conv_s[None])
```

```python
import functools

import jax
import jax.numpy as jnp
from jax import lax
from jax.experimental import pallas as pl
from jax.experimental.pallas import tpu as pltpu

F32 = jnp.float32
BF16 = jnp.bfloat16
HIGHEST = lax.Precision.HIGHEST

D_MODEL = 1024
SEQ = 16384
N_SAMPLE = 32
PAST_LEN = 16384
DEPTH = 2
A_HEADS = 16
A_HEAD_DIM = 64
DILATIONS = (1, 4, 16)
WINDOWS = (128, 512, 2048)
ROPE_THETA = 10000.0
B_HEADS = 4
B_QK = 128
B_V = 256
N_EXPERTS = 32
TOP_K = 4
SWIGLU_LIMIT = 7.0
SWIGLU_ALPHA = 1.702
DN_ALPHA = (2 * DEPTH) ** 0.25
LN_EPS = 1e-5

LANES = 128
TM = 256
NT_P = SEQ // TM
NP = SEQ + TM
NT = NP // TM
ATT_TILE = 2048
QB = 128
N_ATT_TILES = SEQ // ATT_TILE
TAIL_T0 = NT_P - ATT_TILE // TM
EXP_TM = 256
N_ASSIGN = NP * TOP_K
N_EXP_TILES = (N_ASSIGN + N_EXPERTS * (EXP_TM - 1)) // EXP_TM + 1
N_SORTED = N_EXP_TILES * EXP_TM
VMEM_LIMIT = 56 * 1024 * 1024


def _cparams(sem):
    return pltpu.CompilerParams(dimension_semantics=sem, vmem_limit_bytes=VMEM_LIMIT)


def _layer_norm(z, g, b):
    mu = jnp.mean(z, axis=-1, keepdims=True)
    zc = z - mu
    var = jnp.mean(zc * zc, axis=-1, keepdims=True)
    return zc * lax.rsqrt(var + LN_EPS) * g + b


def _rope(res, cos, sin, n_lane_tiles):
    rows = res.shape[0]
    lane = lax.broadcasted_iota(jnp.int32, (rows, LANES), 1)
    first = (lane % A_HEAD_DIM) < (A_HEAD_DIM // 2)
    outs = []
    for c in range(n_lane_tiles):
        seg = res[:, c * LANES:(c + 1) * LANES]
        rot = jnp.where(first, pltpu.roll(seg, LANES - 32, 1), pltpu.roll(seg, 32, 1))
        outs.append(seg * cos + rot * sin)
    return outs


def _class_major(ref, lead, d):
    parts = []
    for r in range(d):
        rows = slice(None) if d == 1 else pl.ds(r, TM // d, stride=d)
        parts.append(ref[rows, :] if lead is None else ref[lead, rows, :])
    return parts[0] if d == 1 else jnp.concatenate(parts, axis=0)


def _proj_a_kernel(x_ref, w_ref, cos_ref, sin_ref, o0_ref, o1_ref, o2_ref):
    j = pl.program_id(0)
    rotated = j < 2
    scale = jnp.where(j == 0, A_HEAD_DIM ** -0.5, 1.0).astype(F32)
    per_group = D_MODEL // LANES
    for g, (d, o_ref) in enumerate(zip(DILATIONS, (o0_ref, o1_ref, o2_ref))):
        xg = jnp.concatenate([_class_major(x_ref, c, d) for c in range(per_group)], axis=1)
        res = jnp.dot(xg.astype(BF16), w_ref[:, g * D_MODEL:(g + 1) * D_MODEL],
                      preferred_element_type=F32)
        cos = jnp.where(rotated, _class_major(cos_ref, None, d) * scale, 1.0)
        sin = jnp.where(rotated, _class_major(sin_ref, None, d) * scale, 0.0)
        out = jnp.concatenate(_rope(res, cos, sin, per_group), axis=1).astype(BF16)
        n = TM // d
        for r in range(d):
            o_ref[0, 0, r] = out[r * n:(r + 1) * n]


def _proj_a(x_slab, w_bf, cos, sin):
    outs = []
    specs = []
    for d in DILATIONS:
        outs.append(jax.ShapeDtypeStruct((3, N_ATT_TILES, d, ATT_TILE // d, D_MODEL), BF16))
        specs.append(pl.BlockSpec((1, 1, d, TM // d, D_MODEL),
                                  lambda j, m: (j, m // (ATT_TILE // TM), 0, m % (ATT_TILE // TM), 0)))
    return pl.pallas_call(
        _proj_a_kernel,
        out_shape=outs,
        grid=(3, NT_P),
        in_specs=[
            pl.BlockSpec((D_MODEL // LANES, TM, LANES), lambda j, m: (0, m, 0)),
            pl.BlockSpec((D_MODEL, 3 * D_MODEL), lambda j, m: (0, j)),
            pl.BlockSpec((TM, LANES), lambda j, m: (m, 0)),
            pl.BlockSpec((TM, LANES), lambda j, m: (m, 0)),
        ],
        out_specs=specs,
        compiler_params=_cparams(("arbitrary", "arbitrary")),
        name="proj_a",
    )(x_slab, w_bf, cos, sin)


def _proj_a_tail_kernel(x_ref, w_ref, cos_ref, sin_ref, o_ref):
    j = pl.program_id(0)
    xb = x_ref[...].astype(BF16)
    res = jnp.dot(xb, w_ref[...], preferred_element_type=F32)
    n_tiles = 3 * D_MODEL // LANES

    @pl.when(j < 2)
    def _():
        scale = jnp.where(j == 0, A_HEAD_DIM ** -0.5, 1.0).astype(F32)
        segs = _rope(res, cos_ref[...], sin_ref[...], n_tiles)
        for c in range(n_tiles):
            o_ref[0, :, c * LANES:(c + 1) * LANES] = segs[c] * scale

    @pl.when(j == 2)
    def _():
        o_ref[0] = res


def _proj_a_tail(x_all, w_bf, cos, sin):
    n = NT - TAIL_T0
    return pl.pallas_call(
        _proj_a_tail_kernel,
        out_shape=jax.ShapeDtypeStruct((3, n * TM, 3 * D_MODEL), F32),
        grid=(3, n),
        in_specs=[
            pl.BlockSpec((TM, D_MODEL), lambda j, m: (m + TAIL_T0, 0)),
            pl.BlockSpec((D_MODEL, 3 * D_MODEL), lambda j, m: (0, j)),
            pl.BlockSpec((TM, LANES), lambda j, m: (m + TAIL_T0, 0)),
            pl.BlockSpec((TM, LANES), lambda j, m: (m + TAIL_T0, 0)),
        ],
        out_specs=pl.BlockSpec((1, TM, 3 * D_MODEL), lambda j, m: (j, m, 0)),
        compiler_params=_cparams(("arbitrary", "arbitrary")),
        name="proj_a_tail",
    )(x_all, w_bf, cos, sin)


def _attn_kernel(*refs):
    in_refs = refs[:15]
    o_ref, oscr, lscr = refs[15:]
    jt = pl.program_id(0)
    lane = lax.broadcasted_iota(jnp.int32, (QB, LANES), 1)
    lo = lane < A_HEAD_DIM
    row = lax.broadcasted_iota(jnp.int32, (QB, 2 * QB), 0)
    col = lax.broadcasted_iota(jnp.int32, (QB, 2 * QB), 1)
    mask_cur = (col >= QB) & (col - QB <= row)
    mask_prev = (col < QB) & (col >= row)
    neg = jnp.float32(-jnp.inf)
    bias_full = jnp.concatenate([jnp.where(mask_cur | mask_prev, 0.0, neg)] * 2, axis=0)
    bias_first = jnp.concatenate([jnp.where(mask_cur, 0.0, neg)] * 2, axis=0)
    n_blocks = ATT_TILE // QB
    zero = jnp.zeros((QB, LANES), BF16)

    for g, d in enumerate(DILATIONS):
        q_ref, kc_ref, kp_ref, vc_ref, vp_ref = in_refs[5 * g:5 * g + 5]
        per_class = n_blocks // d
        for b in range(n_blocks):
            r, c = divmod(b, per_class)
            rows = pl.ds(c * QB, QB)
            q = q_ref[0, 0, r, rows, :]
            kc = kc_ref[0, 0, r, rows, :]
            vc = vc_ref[0, 0, r, rows, :]
            if c == 0:
                prow = pl.ds((per_class - 1) * QB, QB)
                kp = kp_ref[0, 0, r, prow, :]
                vp = vp_ref[0, 0, r, prow, :]
                bias = jnp.where(jt > 0, bias_full, bias_first)
            else:
                prow = pl.ds((c - 1) * QB, QB)
                kp = kc_ref[0, 0, r, prow, :]
                vp = vc_ref[0, 0, r, prow, :]
                bias = bias_full
            lhs = jnp.concatenate([jnp.where(lo, q, zero), jnp.where(lo, zero, q)], axis=0)
            k2 = jnp.concatenate([kp, kc], axis=0)
            v2 = jnp.concatenate([vp, vc], axis=0)
            s = lax.dot_general(lhs, k2, (((1,), (1,)), ((), ())), preferred_element_type=F32)
            s = s + bias
            mx = jnp.max(s, axis=1, keepdims=True)
            p = jnp.exp(s - mx)
            den = jnp.sum(p, axis=1, keepdims=True)
            o2 = jnp.dot(p.astype(BF16), v2, preferred_element_type=F32) / den
            lse = mx + jnp.log(den)
            o = jnp.where(lo, o2[:QB], o2[QB:])
            l = jnp.where(lo, jnp.broadcast_to(lse[:QB], (QB, LANES)),
                          jnp.broadcast_to(lse[QB:], (QB, LANES)))
            if d == 1:
                dst = pl.ds(b * QB, QB)
            else:
                dst = pl.ds(c * QB * d + r, QB, stride=d)
            oscr[g, dst, :] = o
            lscr[g, dst, :] = l

    def merge(i, carry):
        rows = pl.ds(pl.multiple_of(i * QB, QB), QB)
        l0, l1, l2 = lscr[0, rows, :], lscr[1, rows, :], lscr[2, rows, :]
        mx = jnp.maximum(jnp.maximum(l0, l1), l2)
        w0, w1, w2 = jnp.exp(l0 - mx), jnp.exp(l1 - mx), jnp.exp(l2 - mx)
        acc = oscr[0, rows, :] * w0 + oscr[1, rows, :] * w1 + oscr[2, rows, :] * w2
        o_ref[rows, :] = (acc / (w0 + w1 + w2)).astype(o_ref.dtype)
        return carry

    lax.fori_loop(0, n_blocks, merge, 0)


def _attn_prompt(qkv_groups):
    ins, specs = [], []
    for g, d in enumerate(DILATIONS):
        arr = qkv_groups[g]
        blk = (1, 1, d, ATT_TILE // d, LANES)
        cur = lambda s: (lambda j, h: (s, j, 0, 0, h))
        prev = lambda s: (lambda j, h: (s, jnp.maximum(j - 1, 0), 0, 0, h))
        for imap in (cur(0), cur(1), prev(1), cur(2), prev(2)):
            ins.append(arr)
            specs.append(pl.BlockSpec(blk, imap))
    return pl.pallas_call(
        _attn_kernel,
        out_shape=jax.ShapeDtypeStruct((SEQ, D_MODEL), BF16),
        grid=(N_ATT_TILES, D_MODEL // LANES),
        in_specs=specs,
        out_specs=pl.BlockSpec((ATT_TILE, LANES), lambda j, h: (j, h)),
        scratch_shapes=[pltpu.VMEM((3, ATT_TILE, LANES), F32),
                        pltpu.VMEM((3, ATT_TILE, LANES), F32)],
        compiler_params=_cparams(("arbitrary", "arbitrary")),
        name="attn_prompt",
    )(*ins)


SAMPLE_HEADS = 8


def _attn_sample_kernel(qkv_ref, c0_ref, c1_ref, c2_ref, o_ref):
    eye = (lax.broadcasted_iota(jnp.int32, (A_HEAD_DIM, A_HEAD_DIM), 0)
           == lax.broadcasted_iota(jnp.int32, (A_HEAD_DIM, A_HEAD_DIM), 1))

    def to_col(row):
        return jnp.sum(jnp.where(eye, jnp.broadcast_to(row, eye.shape), 0.0), axis=1, keepdims=True)

    def to_row(col):
        return jnp.sum(jnp.where(eye, jnp.broadcast_to(col, eye.shape), 0.0), axis=0, keepdims=True)

    rows = []
    for h in range(SAMPLE_HEADS):
        outs, lses = [], []
        for g, (d, c_ref) in enumerate(zip(DILATIONS, (c0_ref, c1_ref, c2_ref))):
            q = to_col(qkv_ref[0, g, h:h + 1, :])
            kn = to_col(qkv_ref[1, g, h:h + 1, :])
            vn = to_col(qkv_ref[2, g, h:h + 1, :])
            kt = c_ref[0, h]
            vt = c_ref[1, h]
            pos = lax.broadcasted_iota(jnp.int32, (1, kt.shape[1]), 1)
            s = jnp.sum(kt * q, axis=0, keepdims=True)
            s = jnp.where((pos & (d - 1)) == 0, s, -jnp.inf)
            s_self = jnp.sum(q * kn, axis=0, keepdims=True)
            mx = jnp.maximum(jnp.max(s, axis=1, keepdims=True), s_self)
            p = jnp.exp(s - mx)
            p_self = jnp.exp(s_self - mx)
            den = jnp.sum(p, axis=1, keepdims=True) + p_self
            acc = jnp.sum(vt * p, axis=1, keepdims=True) + p_self * vn
            outs.append(acc / den)
            lses.append(mx + jnp.log(den))
        mx = jnp.maximum(jnp.maximum(lses[0], lses[1]), lses[2])
        ws = [jnp.exp(l - mx) for l in lses]
        acc = outs[0] * ws[0] + outs[1] * ws[1] + outs[2] * ws[2]
        rows.append(to_row(acc / (ws[0] + ws[1] + ws[2])))
    o_ref[...] = jnp.concatenate(rows, axis=0)


def _attn_sample(qkv_s, caches_t):
    hg = SAMPLE_HEADS
    specs = [pl.BlockSpec((None, 3, 3, hg, A_HEAD_DIM), lambda b, h: (b, 0, 0, h, 0))]
    for c in caches_t:
        specs.append(pl.BlockSpec((None, 2, hg, A_HEAD_DIM, c.shape[-1]),
                                  lambda b, h: (b, 0, h, 0, 0)))
    return pl.pallas_call(
        _attn_sample_kernel,
        out_shape=jax.ShapeDtypeStruct((N_SAMPLE, A_HEADS, A_HEAD_DIM), F32),
        grid=(N_SAMPLE, A_HEADS // hg),
        in_specs=specs,
        out_specs=pl.BlockSpec((None, hg, A_HEAD_DIM), lambda b, h: (b, h, 0)),
        compiler_params=_cparams(("arbitrary", "arbitrary")),
        name="attn_sample",
    )(qkv_s, *caches_t)


def _out_proj_ln_kernel(yp_ref, ys_ref, x_ref, w_ref, g_ref, b_ref, o_ref):
    m = pl.program_id(0)
    y = jnp.where(m < NT_P, yp_ref[...], ys_ref[...])
    proj = jnp.dot(y, w_ref[...], preferred_element_type=F32)
    o_ref[...] = _layer_norm(DN_ALPHA * x_ref[...] + proj, g_ref[...], b_ref[...])


def _out_proj_ln(y_p, y_s, x_all, w_bf, g, b):
    return pl.pallas_call(
        _out_proj_ln_kernel,
        out_shape=jax.ShapeDtypeStruct((NP, D_MODEL), F32),
        grid=(NT,),
        in_specs=[
            pl.BlockSpec((TM, D_MODEL), lambda m: (jnp.minimum(m, NT_P - 1), 0)),
            pl.BlockSpec((TM, D_MODEL), lambda m: (0, 0)),
            pl.BlockSpec((TM, D_MODEL), lambda m: (m, 0)),
            pl.BlockSpec((D_MODEL, D_MODEL), lambda m: (0, 0)),
            pl.BlockSpec((1, D_MODEL), lambda m: (0, 0)),
            pl.BlockSpec((1, D_MODEL), lambda m: (0, 0)),
        ],
        out_specs=pl.BlockSpec((TM, D_MODEL), lambda m: (m, 0)),
        compiler_params=_cparams(("arbitrary",)),
        name="out_proj_ln",
    )(y_p, y_s, x_all, w_bf, g, b)


def _router_kernel(x_ref, w_ref, b_ref, route_ref, gate_ref, cnt_ref, run_scr):
    m = pl.program_id(0)

    @pl.when(m == 0)
    def _():
        run_scr[...] = jnp.zeros_like(run_scr)

    logits = jnp.dot(x_ref[...], w_ref[...], precision=HIGHEST,
                     preferred_element_type=F32) + b_ref[...]
    lane = lax.broadcasted_iota(jnp.int32, (TM, LANES), 1)
    tri = (lax.broadcasted_iota(jnp.int32, (TM, TM), 1)
           < lax.broadcasted_iota(jnp.int32, (TM, TM), 0)).astype(BF16)
    running = run_scr[0:1, :]
    packed = jnp.zeros((TM, LANES), F32)
    vals = jnp.zeros((TM, LANES), F32)
    work = logits
    for k in range(TOP_K):
        best = jnp.max(work, axis=1, keepdims=True)
        idx = jnp.min(jnp.where(work == best, lane, LANES), axis=1, keepdims=True)
        sel = lane == idx
        work = jnp.where(sel, -jnp.inf, work)
        onehot = sel.astype(BF16)
        before = jnp.dot(tri, onehot, preferred_element_type=F32) + running
        rank = jnp.sum(jnp.where(sel, before, 0.0), axis=1, keepdims=True)
        running = running + jnp.sum(onehot.astype(F32), axis=0, keepdims=True)
        packed = jnp.where(lane == k, idx.astype(F32), packed)
        packed = jnp.where(lane == TOP_K + k, rank, packed)
        vals = jnp.where(lane == k, best, vals)
    top = lane < TOP_K
    e = jnp.where(top, jnp.exp(vals - vals[:, 0:1]), 0.0)
    gate_ref[...] = e / jnp.sum(e, axis=1, keepdims=True)
    route_ref[...] = packed.T[:2 * TOP_K, :].astype(jnp.int32)
    run_scr[...] = jnp.broadcast_to(running, run_scr.shape)
    cnt_ref[...] = jnp.broadcast_to(running, cnt_ref.shape)


def _router(x_all, w_pad, b_pad):
    out = [jax.ShapeDtypeStruct((2 * TOP_K, NP), jnp.int32),
           jax.ShapeDtypeStruct((NP, LANES), F32),
           jax.ShapeDtypeStruct((8, LANES), F32)]
    return pl.pallas_call(
        _router_kernel,
        out_shape=out,
        grid=(NT,),
        in_specs=[pl.BlockSpec((TM, D_MODEL), lambda m: (m, 0)),
                  pl.BlockSpec((D_MODEL, LANES), lambda m: (0, 0)),
                  pl.BlockSpec((1, LANES), lambda m: (0, 0))],
        out_specs=[pl.BlockSpec((2 * TOP_K, TM), lambda m: (0, m)),
                   pl.BlockSpec((TM, LANES), lambda m: (m, 0)),
                   pl.BlockSpec((8, LANES), lambda m: (0, 0))],
        scratch_shapes=[pltpu.VMEM((8, LANES), F32)],
        compiler_params=_cparams(("arbitrary",)),
        name="router",
    )(x_all, w_pad, b_pad)


def _positions_kernel(start_ref, route_ref, pos_ref):
    ids = route_ref[0:TOP_K, :]
    acc = route_ref[TOP_K:2 * TOP_K, :]
    for e in range(N_EXPERTS):
        acc = acc + jnp.where(ids == e, start_ref[e], 0)
    pos_ref[...] = acc


def _positions(start, route):
    grid_spec = pltpu.PrefetchScalarGridSpec(
        num_scalar_prefetch=1,
        grid=(1,),
        in_specs=[pl.BlockSpec((2 * TOP_K, NP), lambda i, s: (0, 0))],
        out_specs=pl.BlockSpec((TOP_K, NP), lambda i, s: (0, 0)),
    )
    return pl.pallas_call(
        _positions_kernel,
        out_shape=jax.ShapeDtypeStruct((TOP_K, NP), jnp.int32),
        grid_spec=grid_spec,
        compiler_params=_cparams(("arbitrary",)),
        name="moe_positions",
    )(start, route)


def _dispatch_kernel(ltile_ref, nv_ref, pos_ref, x_ref, xs_ref, zbuf, sem, zsem):
    m = pl.program_id(0)

    @pl.when(m == 0)
    def _():
        zbuf[...] = jnp.zeros_like(zbuf)

        def zero_copy(tile):
            row = pl.multiple_of(tile * EXP_TM, EXP_TM)
            return pltpu.make_async_copy(zbuf, xs_ref.at[pl.ds(row, EXP_TM), :], zsem)

        for e in range(N_EXPERTS):
            @pl.when(ltile_ref[e] >= 0)
            def _():
                zero_copy(jnp.maximum(ltile_ref[e], 0)).start()

        def start_tail(i, carry):
            zero_copy(i).start()
            return carry

        def wait_tail(i, carry):
            zero_copy(i).wait()
            return carry

        lax.fori_loop(nv_ref[0], N_EXP_TILES, start_tail, 0)
        for e in range(N_EXPERTS):
            @pl.when(ltile_ref[e] >= 0)
            def _():
                zero_copy(jnp.maximum(ltile_ref[e], 0)).wait()
        lax.fori_loop(nv_ref[0], N_EXP_TILES, wait_tail, 0)

    def issue(t, carry):
        for k in range(TOP_K):
            pltpu.make_async_copy(x_ref.at[pl.ds(t, 1), :],
                                  xs_ref.at[pl.ds(pos_ref[k, t], 1), :],
                                  sem).start(priority=k % 2)
        return carry

    lax.fori_loop(0, TM, issue, 0, unroll=8)
    for k in range(TOP_K):
        pltpu.make_async_copy(x_ref, xs_ref.at[pl.ds(0, TM), :], sem).wait()


def _dispatch(last_tile, n_valid, pos, x_all):
    grid_spec = pltpu.PrefetchScalarGridSpec(
        num_scalar_prefetch=2,
        grid=(NT,),
        in_specs=[pl.BlockSpec((TOP_K, TM), lambda m, *_: (0, m), memory_space=pltpu.SMEM),
                  pl.BlockSpec((TM, D_MODEL), lambda m, *_: (m, 0))],
        out_specs=pl.BlockSpec(memory_space=pl.ANY),
        scratch_shapes=[pltpu.VMEM((EXP_TM, D_MODEL), F32),
                        pltpu.SemaphoreType.DMA, pltpu.SemaphoreType.DMA],
    )
    return pl.pallas_call(
        _dispatch_kernel,
        out_shape=jax.ShapeDtypeStruct((N_SORTED, D_MODEL), F32),
        grid_spec=grid_spec,
        compiler_params=_cparams(("arbitrary",)),
        name="moe_dispatch",
    )(last_tile, n_valid, pos, x_all)


def _experts_kernel(te_ref, nv_ref, nxt_ref, x_ref, bgu_ref, bdn_ref, wgu_hbm, wdn_hbm, o_ref,
                    stage_gu, stage_dn, wgu_bf, wdn_bf, sem, *, layer):
    i = pl.program_id(0)
    expert = te_ref[i]
    fresh = (i == 0) | (expert != te_ref[jnp.maximum(i - 1, 0)])

    def fetch(e):
        return (pltpu.make_async_copy(wgu_hbm.at[layer, e], stage_gu, sem.at[0]),
                pltpu.make_async_copy(wdn_hbm.at[layer, e], stage_dn, sem.at[1]))

    @pl.when(i == 0)
    def _():
        for copy in fetch(expert):
            copy.start()

    @pl.when(fresh)
    def _():
        for copy in fetch(expert):
            copy.wait()
        wgu_bf[...] = stage_gu[...].astype(BF16)
        wdn_bf[...] = stage_dn[...].astype(BF16)
        nxt = nxt_ref[expert]

        @pl.when(nxt >= 0)
        def _():
            for copy in fetch(nxt):
                copy.start()

    @pl.when(i < nv_ref[0])
    def _():
        xb = x_ref[...].astype(BF16)
        h = jnp.dot(xb, wgu_bf[...], preferred_element_type=F32) + bgu_ref[0]
        hg = jnp.minimum(h[:, :D_MODEL], SWIGLU_LIMIT)
        hu = jnp.clip(h[:, D_MODEL:], -SWIGLU_LIMIT, SWIGLU_LIMIT)
        act = hg * jax.nn.sigmoid(SWIGLU_ALPHA * hg) * (hu + 1.0)
        o_ref[...] = jnp.dot(act.astype(BF16), wdn_bf[...],
                             preferred_element_type=F32) + bdn_ref[0]

    @pl.when(i >= nv_ref[0])
    def _():
        o_ref[...] = jnp.zeros_like(o_ref)


def _experts(layer, tile_expert, n_valid, next_expert, x_sorted, w_gu, b_gu, w_dn, b_dn):
    def row_map(i, te, nv, nx):
        return (jnp.maximum(jnp.minimum(i, nv[0] - 1), 0), 0)

    def expert_map(i, te, nv, nx):
        return (layer, te[i], 0, 0)

    grid_spec = pltpu.PrefetchScalarGridSpec(
        num_scalar_prefetch=3,
        grid=(N_EXP_TILES,),
        in_specs=[
            pl.BlockSpec((EXP_TM, D_MODEL), row_map),
            pl.BlockSpec((None, 1, 1, 2 * D_MODEL), expert_map),
            pl.BlockSpec((None, 1, 1, D_MODEL), expert_map),
            pl.BlockSpec(memory_space=pl.ANY),
            pl.BlockSpec(memory_space=pl.ANY),
        ],
        out_specs=pl.BlockSpec((EXP_TM, D_MODEL), lambda i, te, nv, nx: (i, 0)),
        scratch_shapes=[pltpu.VMEM((D_MODEL, 2 * D_MODEL), F32),
                        pltpu.VMEM((D_MODEL, D_MODEL), F32),
                        pltpu.VMEM((D_MODEL, 2 * D_MODEL), BF16),
                        pltpu.VMEM((D_MODEL, D_MODEL), BF16),
                        pltpu.SemaphoreType.DMA((2,))],
    )
    return pl.pallas_call(
        functools.partial(_experts_kernel, layer=layer),
        out_shape=jax.ShapeDtypeStruct((N_SORTED, D_MODEL), F32),
        grid_spec=grid_spec,
        compiler_params=_cparams(("arbitrary",)),
        name="moe_experts",
    )(tile_expert, n_valid, next_expert, x_sorted, b_gu, b_dn, w_gu, w_dn)


CMB_TM = 128
N_CMB = NP // CMB_TM


def _combine_ln_kernel(pos_ref, nxt_ref, gate_ref, x_ref, g_ref, b_ref, ys_ref, o_ref, buf, sem):
    m = pl.program_id(0)

    def fetch(idx_ref, slot):
        def issue(t, carry):
            for k in range(TOP_K):
                pltpu.make_async_copy(ys_ref.at[pl.ds(idx_ref[k, t], 1), :],
                                      buf.at[slot, k, pl.ds(t, 1), :],
                                      sem.at[slot]).start(priority=k % 2)
            return carry

        lax.fori_loop(0, CMB_TM, issue, 0, unroll=8)

    @pl.when(m == 0)
    def _():
        fetch(pos_ref, 0)

    @pl.when(m + 1 < N_CMB)
    def _():
        fetch(nxt_ref, (m + 1) % 2)

    slot = m % 2
    pltpu.make_async_copy(buf.at[slot], buf.at[slot], sem.at[slot]).wait()
    gate = gate_ref[...]
    acc = buf[slot, 0] * gate[:, 0:1]
    for k in range(1, TOP_K):
        acc = acc + buf[slot, k] * gate[:, k:k + 1]
    o_ref[...] = _layer_norm(DN_ALPHA * x_ref[...] + acc, g_ref[...], b_ref[...])


def _combine_ln(pos, gate, x_all, g, b, y_sorted):
    idx = lambda off: pl.BlockSpec((TOP_K, CMB_TM),
                                   lambda m: (0, jnp.minimum(m + off, N_CMB - 1)),
                                   memory_space=pltpu.SMEM)
    return pl.pallas_call(
        _combine_ln_kernel,
        out_shape=jax.ShapeDtypeStruct((NP, D_MODEL), F32),
        grid=(N_CMB,),
        in_specs=[idx(0), idx(1),
                  pl.BlockSpec((CMB_TM, LANES), lambda m: (m, 0)),
                  pl.BlockSpec((CMB_TM, D_MODEL), lambda m: (m, 0)),
                  pl.BlockSpec((1, D_MODEL), lambda m: (0, 0)),
                  pl.BlockSpec((1, D_MODEL), lambda m: (0, 0)),
                  pl.BlockSpec(memory_space=pl.ANY)],
        out_specs=pl.BlockSpec((CMB_TM, D_MODEL), lambda m: (m, 0)),
        scratch_shapes=[pltpu.VMEM((2, TOP_K, CMB_TM, D_MODEL), F32),
                        pltpu.SemaphoreType.DMA((2,))],
        compiler_params=_cparams(("arbitrary",)),
        name="moe_combine_ln",
    )(pos, pos, gate, x_all, g, b, y_sorted)


def _moe_layer(layer, x_all, w_r, b_r, w_gu, b_gu, w_dn, b_dn, g, b):
    w_pad = jnp.pad(w_r, ((0, 0), (0, LANES - N_EXPERTS)))
    b_pad = jnp.pad(b_r, (0, LANES - N_EXPERTS), constant_values=-jnp.inf)[None]
    route, gate, counts = _router(x_all, w_pad, b_pad)
    counts = counts[0, :N_EXPERTS].astype(jnp.int32)
    tiles = (counts + EXP_TM - 1) // EXP_TM
    tile_end = jnp.cumsum(tiles)
    start = ((tile_end - tiles) * EXP_TM).astype(jnp.int32)
    last_tile = jnp.where(tiles > 0, tile_end - 1, -1).astype(jnp.int32)
    n_valid = tile_end[-1:].astype(jnp.int32)
    tile_ids = jnp.arange(N_EXP_TILES, dtype=jnp.int32)
    tile_expert = jnp.minimum(
        jnp.sum((tile_ids[:, None] >= tile_end[None, :]).astype(jnp.int32), axis=1),
        N_EXPERTS - 1).astype(jnp.int32)
    tile_expert = jnp.where(tile_ids < n_valid[0], tile_expert,
                            jnp.take(tile_expert, jnp.maximum(n_valid[0] - 1, 0)))
    experts = jnp.arange(N_EXPERTS, dtype=jnp.int32)
    later = (tiles > 0)[None, :] & (experts[None, :] > experts[:, None])
    next_expert = jnp.min(jnp.where(later, experts[None, :], N_EXPERTS), axis=1)
    next_expert = jnp.where(next_expert == N_EXPERTS, -1, next_expert).astype(jnp.int32)
    pos = _positions(start, route)
    x_sorted = _dispatch(last_tile, n_valid, pos, x_all)
    y_sorted = _experts(layer, tile_expert, n_valid, next_expert, x_sorted, w_gu, b_gu, w_dn,
                        b_dn)
    return _combine_ln(pos, gate, x_all, g[None], b[None], y_sorted)


CONV_K = 4
HK = B_HEADS * B_QK
HV = B_HEADS * B_V


def _conv_silu(prev3, cur, cw, cb):
    rows = cur.shape[0]
    ext = jnp.concatenate([jnp.zeros((5, 2 * HK), F32), prev3, cur], axis=0)
    acc = cb + cw[CONV_K - 1:CONV_K] * cur
    for j in range(CONV_K - 1):
        acc = acc + cw[j:j + 1] * ext[5 + j:5 + j + rows]
    return acc * jax.nn.sigmoid(acc)


def _proj_b_kernel(x_ref, w_ref, wg_ref, gb_ref, cw_ref, cb_ref,
                   q_ref, k_ref, v_ref, op_ref, gt_ref, conv_ref, carry):
    m = pl.program_id(0)

    @pl.when(m == 0)
    def _():
        carry[...] = jnp.zeros_like(carry)

    xb = x_ref[...].astype(BF16)
    p = jnp.dot(xb, w_ref[...], preferred_element_type=F32)
    gates = jnp.dot(xb, wg_ref[...], preferred_element_type=F32) + gb_ref[...]
    lane = lax.broadcasted_iota(jnp.int32, gates.shape, 1)
    is_f = (lane >= B_HEADS) & (lane < 2 * B_HEADS)
    gt_ref[...] = jnp.where(is_f, jax.nn.log_sigmoid(gates), gates)
    qk_pre = p[:, :2 * HK]
    qk = _conv_silu(carry[5:8, :], qk_pre, cw_ref[...], cb_ref[...])
    q_ref[...] = qk[:, :HK].astype(BF16)
    k_ref[...] = (qk[:, HK:] * (B_QK ** -0.5)).astype(BF16)
    v_ref[...] = p[:, 2 * HK:2 * HK + HV].astype(BF16)
    op_ref[...] = p[:, 2 * HK + HV:]
    carry[...] = qk_pre[TM - 8:, :]
    conv_ref[...] = qk_pre[TM - 8:, :]


def _proj_b(x_all, w_bf, wg_bf, gb, cw, cb):
    row = lambda n: pl.BlockSpec((TM, n), lambda m: (m, 0))
    const = lambda s: pl.BlockSpec(s, lambda m: (0, 0))
    out = [jax.ShapeDtypeStruct((SEQ, HK), BF16), jax.ShapeDtypeStruct((SEQ, HK), BF16),
           jax.ShapeDtypeStruct((SEQ, HV), BF16), jax.ShapeDtypeStruct((SEQ, HV), F32),
           jax.ShapeDtypeStruct((SEQ, LANES), F32), jax.ShapeDtypeStruct((8, 2 * HK), F32)]
    return pl.pallas_call(
        _proj_b_kernel,
        out_shape=out,
        grid=(NT_P,),
        in_specs=[row(D_MODEL), const((D_MODEL, 3 * D_MODEL)), const((D_MODEL, LANES)),
                  const((1, LANES)), const((CONV_K, 2 * HK)), const((1, 2 * HK))],
        out_specs=[row(HK), row(HK), row(HV), row(HV), row(LANES), const((8, 2 * HK))],
        scratch_shapes=[pltpu.VMEM((8, 2 * HK), F32)],
        compiler_params=_cparams(("arbitrary",)),
        name="proj_b",
    )(x_all, w_bf, wg_bf, gb, cw, cb)


CHUNK = 128


def _mlstm_kernel(q_ref, k_ref, v_ref, op_ref, gt_ref, nw_ref,
                  h_ref, c_out, n_out, m_out, c_scr, n_scr, m_scr):
    ci = pl.program_id(0)

    @pl.when(ci == 0)
    def _():
        c_scr[...] = jnp.zeros_like(c_scr)
        n_scr[...] = jnp.zeros_like(n_scr)
        m_scr[...] = jnp.zeros_like(m_scr)

    gates = gt_ref[...]
    t_idx = lax.broadcasted_iota(jnp.int32, (CHUNK, CHUNK), 0)
    s_idx = lax.broadcasted_iota(jnp.int32, (CHUNK, CHUNK), 1)
    causal = s_idx <= t_idx
    tri = causal.astype(F32)
    b_cols = jnp.dot(tri, gates, precision=HIGHEST, preferred_element_type=F32)
    gates_t = gates.T
    b_rows = jnp.dot(gates_t, (t_idx <= s_idx).astype(F32), precision=HIGHEST,
                     preferred_element_type=F32)
    for h in range(B_HEADS):
        q = q_ref[:, h * B_QK:(h + 1) * B_QK]
        k = k_ref[:, h * B_QK:(h + 1) * B_QK]
        v = v_ref[:, h * B_V:(h + 1) * B_V]
        m_prev = m_scr[h:h + 1, 0:1]
        n_prev = n_scr[h:h + 1, :]
        c_prev = c_scr[h]
        b_col = b_cols[:, B_HEADS + h:B_HEADS + h + 1]
        i_col = gates[:, h:h + 1]
        b_row = b_rows[B_HEADS + h:B_HEADS + h + 1, :]
        i_row = gates_t[h:h + 1, :]
        log_d = jnp.where(causal, b_col + (i_row - b_row), -jnp.inf)
        m_t = jnp.maximum(b_col + m_prev, jnp.max(log_d, axis=1, keepdims=True))
        dmat = jnp.exp(log_d - m_t)
        inter = jnp.exp(b_col + m_prev - m_t)
        s = lax.dot_general(q, k, (((1,), (1,)), ((), ())), preferred_element_type=F32) * dmat
        qf = q.astype(F32)
        num = (jnp.dot(s.astype(BF16), v, preferred_element_type=F32)
               + inter * jnp.dot(q, c_prev.astype(BF16), preferred_element_type=F32))
        den = (jnp.sum(s, axis=1, keepdims=True)
               + inter * jnp.sum(qf * n_prev, axis=1, keepdims=True))
        hv = num / jnp.maximum(jnp.abs(den), jnp.exp(-m_t))
        m_new = m_t[CHUNK - 1:CHUNK, :]
        b_last = b_col[CHUNK - 1:CHUNK, :]
        decay = jnp.exp(b_last - b_col + i_col - m_new)
        gfac = jnp.exp(b_last + m_prev - m_new)
        kd = k.astype(F32) * decay
        c_scr[h] = gfac * c_prev + jnp.dot(kd.T.astype(BF16), v, preferred_element_type=F32)
        n_scr[h:h + 1, :] = gfac * n_prev + jnp.sum(kd, axis=0, keepdims=True)
        m_scr[h:h + 1, :] = jnp.broadcast_to(m_new, (1, LANES))
        mu = jnp.mean(hv, axis=1, keepdims=True)
        hc = hv - mu
        var = jnp.mean(hc * hc, axis=1, keepdims=True)
        hn = hc * lax.rsqrt(var + LN_EPS)
        cols = slice(h * B_V, (h + 1) * B_V)
        hn = hn * nw_ref[:, cols] * jax.nn.sigmoid(op_ref[:, cols])
        h_ref[:, cols] = hn.astype(h_ref.dtype)
    c_out[...] = c_scr[...]
    n_out[...] = n_scr[...]
    m_out[...] = m_scr[...]


def _mlstm_prompt(q, k, v, o_pre, gates, norm_w):
    row = lambda n: pl.BlockSpec((CHUNK, n), lambda c: (c, 0))
    out = [jax.ShapeDtypeStruct((SEQ, HV), BF16),
           jax.ShapeDtypeStruct((B_HEADS, B_QK, B_V), F32),
           jax.ShapeDtypeStruct((8, LANES), F32),
           jax.ShapeDtypeStruct((8, LANES), F32)]
    return pl.pallas_call(
        _mlstm_kernel,
        out_shape=out,
        grid=(SEQ // CHUNK,),
        in_specs=[row(HK), row(HK), row(HV), row(HV), row(LANES),
                  pl.BlockSpec((1, HV), lambda c: (0, 0))],
        out_specs=[row(HV),
                   pl.BlockSpec((B_HEADS, B_QK, B_V), lambda c: (0, 0, 0)),
                   pl.BlockSpec((8, LANES), lambda c: (0, 0)),
                   pl.BlockSpec((8, LANES), lambda c: (0, 0))],
        scratch_shapes=[pltpu.VMEM((B_HEADS, B_QK, B_V), F32),
                        pltpu.VMEM((8, LANES), F32), pltpu.VMEM((8, LANES), F32)],
        compiler_params=_cparams(("arbitrary",)),
        name="mlstm_prompt",
    )(q, k, v, o_pre, gates, norm_w)


def _mlstm_sample_kernel(x_ref, w_ref, wg_ref, gb_ref, cw_ref, cb_ref, conv_ref,
                         c_ref, n_ref, m_ref, nw_ref,
                         h_ref, conv_out, c_out, n_out, m_out):
    xb = x_ref[...].astype(BF16)
    p = jnp.dot(xb, w_ref[...], preferred_element_type=F32)
    gates = jnp.dot(xb, wg_ref[...], preferred_element_type=F32) + gb_ref[...]
    qk_pre = p[0:1, :2 * HK]
    prev = conv_ref[0]
    cw = cw_ref[...]
    acc = cb_ref[...] + cw[3:4] * qk_pre
    for j in range(CONV_K - 1):
        acc = acc + cw[j:j + 1] * prev[j:j + 1]
    qk = acc * jax.nn.sigmoid(acc)
    conv_out[0] = jnp.concatenate([prev[1:3], qk_pre], axis=0)
    v_all = p[0:1, 2 * HK:2 * HK + HV]
    o_pre = p[0:1, 2 * HK + HV:]
    ones8 = jnp.ones((8, 1), F32)
    for h in range(B_HEADS):
        q = qk[:, h * B_QK:(h + 1) * B_QK]
        k = qk[:, HK + h * B_QK:HK + (h + 1) * B_QK] * (B_QK ** -0.5)
        v = v_all[:, h * B_V:(h + 1) * B_V]
        i_pre = gates[0:1, h:h + 1]
        log_f = jax.nn.log_sigmoid(gates[0:1, B_HEADS + h:B_HEADS + h + 1])
        m_prev = m_ref[0, h:h + 1, 0:1]
        n_prev = n_ref[0, h:h + 1, :]
        c_prev = c_ref[0, h]
        m_t = jnp.maximum(log_f + m_prev, i_pre)
        dgate = jnp.exp(i_pre - m_t)
        inter = jnp.exp(log_f + m_prev - m_t)
        s = jnp.sum(q * k, axis=1, keepdims=True) * dgate
        cq = lax.dot_general(ones8 * q, c_prev, (((1,), (1,)), ((), ())),
                             precision=HIGHEST, preferred_element_type=F32)[0:1]
        num = s * v + inter * cq
        den = s + inter * jnp.sum(n_prev * q, axis=1, keepdims=True)
        hv = num / jnp.maximum(jnp.abs(den), jnp.exp(-m_t))
        v_cols = jnp.broadcast_to(v, (B_QK, B_V)).T
        c_out[0, h] = inter * c_prev + dgate * (v_cols * k)
        n_out[0, h:h + 1, :] = inter * n_prev + dgate * k
        m_out[0, h:h + 1, :] = jnp.broadcast_to(m_t, (1, LANES))
        mu = jnp.mean(hv, axis=1, keepdims=True)
        hc = hv - mu
        var = jnp.mean(hc * hc, axis=1, keepdims=True)
        cols = slice(h * B_V, (h + 1) * B_V)
        hn = hc * lax.rsqrt(var + LN_EPS) * nw_ref[:, cols] * jax.nn.sigmoid(o_pre[:, cols])
        h_ref[0, :, cols] = jnp.broadcast_to(hn, (8, B_V))


def _mlstm_sample(x_s8, w_bf, wg_bf, gb, cw, cb, conv_state, c_state, n_state, m_state, norm_w):
    const = lambda s: pl.BlockSpec(s, lambda b: (0,) * len(s))
    per = lambda s: pl.BlockSpec((1,) + s, lambda b: (b,) + (0,) * len(s))
    out = [jax.ShapeDtypeStruct((N_SAMPLE, 8, HV), F32),
           jax.ShapeDtypeStruct((N_SAMPLE, 3, 2 * HK), F32),
           jax.ShapeDtypeStruct((N_SAMPLE, B_HEADS, B_V, B_QK), F32),
           jax.ShapeDtypeStruct((N_SAMPLE, B_HEADS, B_QK), F32),
           jax.ShapeDtypeStruct((N_SAMPLE, B_HEADS, LANES), F32)]
    return pl.pallas_call(
        _mlstm_sample_kernel,
        out_shape=out,
        grid=(N_SAMPLE,),
        in_specs=[pl.BlockSpec((8, D_MODEL), lambda b: (b, 0)),
                  const((D_MODEL, 3 * D_MODEL)), const((D_MODEL, LANES)), const((1, LANES)),
                  const((CONV_K, 2 * HK)), const((1, 2 * HK)),
                  per((3, 2 * HK)), per((B_HEADS, B_V, B_QK)), per((B_HEADS, B_QK)),
                  per((B_HEADS, LANES)), const((1, HV))],
        out_specs=[per((8, HV)), per((3, 2 * HK)), per((B_HEADS, B_V, B_QK)),
                   per((B_HEADS, B_QK)), per((B_HEADS, LANES))],
        compiler_params=_cparams(("arbitrary",)),
        name="mlstm_sample",
    )(x_s8, w_bf, wg_bf, gb, cw, cb, conv_state, c_state, n_state, m_state, norm_w)


def _rope_tables():
    half = A_HEAD_DIM // 2
    pos = jnp.concatenate([jnp.arange(SEQ, dtype=jnp.int32),
                           jnp.full((TM,), PAST_LEN, jnp.int32)])
    inv = ROPE_THETA ** (-jnp.arange(half, dtype=F32) / half)
    ang = pos.astype(F32)[:, None] * inv[None, :]
    cos, sin = jnp.cos(ang), jnp.sin(ang)
    cos_t = jnp.tile(cos, (1, LANES // half))
    sin_t = jnp.tile(jnp.concatenate([-sin, sin], axis=1), (1, LANES // A_HEAD_DIM))
    return cos_t, sin_t


def kernel(x_prompt, x_sample, cache_kv_w128, cache_kv_w512, cache_kv_w2048, state_C, state_n,
           state_m, state_conv, a_w_in, a_w_out, b_w_in, b_conv_w, b_conv_b, b_gate_b, b_norm_w,
           b_w_out, ln_g, ln_b, moe_w_router, moe_b_router, moe_w_gu, moe_b_gu, moe_w_down,
           moe_b_down):
    pad = jnp.zeros((TM - N_SAMPLE, D_MODEL), F32)
    x_all = jnp.concatenate([x_prompt[0], x_sample[:, 0], pad], axis=0)

    cos_t, sin_t = _rope_tables()
    w_in = a_w_in[0].astype(BF16)
    x_slab = jnp.transpose(x_all.reshape(NP, D_MODEL // LANES, LANES), (1, 0, 2))
    groups = _proj_a(x_slab, w_in, cos_t, sin_t)
    tail = _proj_a_tail(x_all, w_in, cos_t, sin_t)
    o_p = _attn_prompt(groups)
    n_tail = ATT_TILE
    qkv_s = tail[:, n_tail:n_tail + N_SAMPLE].reshape(3, N_SAMPLE, 3, A_HEADS, A_HEAD_DIM)
    qkv_s = jnp.transpose(qkv_s, (1, 0, 2, 3, 4))
    caches_t = [jnp.transpose(c[0], (0, 2, 3, 4, 1))
                for c in (cache_kv_w128, cache_kv_w512, cache_kv_w2048)]
    o_s = _attn_sample(qkv_s, caches_t).reshape(N_SAMPLE, D_MODEL)
    o_s = jnp.concatenate([o_s, pad], axis=0).astype(BF16)
    x_all = _out_proj_ln(o_p, o_s, x_all, a_w_out[0].astype(BF16), ln_g[0, 0][None], ln_b[0, 0][None])
    b_gu4, b_dn4 = moe_b_gu[:, :, None, :], moe_b_down[:, :, None, :]
    x_all = _moe_layer(0, x_all, moe_w_router[0], moe_b_router[0], moe_w_gu, b_gu4,
                       moe_w_down, b_dn4, ln_g[0, 1], ln_b[0, 1])

    kv_p, kv_s = [], []
    for g, (d, w) in enumerate(zip(DILATIONS, WINDOWS)):
        k_nat = tail[1, n_tail - w:n_tail, g * D_MODEL:(g + 1) * D_MODEL]
        v_nat = tail[2, n_tail - w:n_tail, g * D_MODEL:(g + 1) * D_MODEL]
        kv = jnp.stack([k_nat, v_nat], axis=1).reshape(1, 1, w, 2, A_HEADS, A_HEAD_DIM)
        kv_p.append(kv)
        kv_s.append(jnp.stack([qkv_s[:, 1, g], qkv_s[:, 2, g]], axis=1)[None, :, None])

    w_b = b_w_in[0]
    w_main = w_b[:, :3 * D_MODEL].astype(BF16)
    w_gate = jnp.pad(w_b[:, 3 * D_MODEL:], ((0, 0), (0, LANES - 2 * B_HEADS))).astype(BF16)
    gb = jnp.pad(b_gate_b[0].reshape(-1), (0, LANES - 2 * B_HEADS))[None]
    cw, cb, nw = b_conv_w[0], b_conv_b[0][None], b_norm_w[0][None]
    q, k, v, o_pre, gates, conv_tail = _proj_b(x_all, w_main, w_gate, gb, cw, cb)
    h_p, c_p, n_p, m_p = _mlstm_prompt(q, k, v, o_pre, gates, nw)
    x_s = x_all[SEQ:SEQ + N_SAMPLE]
    x_s8 = jnp.pad(x_s[:, None, :], ((0, 0), (0, 7), (0, 0))).reshape(N_SAMPLE * 8, D_MODEL)
    m_in = jnp.broadcast_to(state_m[0][:, :, None], (N_SAMPLE, B_HEADS, LANES))
    h_s, conv_s, c_s, n_s, m_s = _mlstm_sample(x_s8, w_main, w_gate, gb, cw, cb, state_conv[0],
                                               state_C[0], state_n[0], m_in, nw)
    h_s = jnp.concatenate([h_s[:, 0, :], pad], axis=0).astype(BF16)
    x_all = _out_proj_ln(h_p, h_s, x_all, b_w_out[0].astype(BF16), ln_g[1, 0][None], ln_b[1, 0][None])
    x_all = _moe_layer(1, x_all, moe_w_router[1], moe_b_router[1], moe_w_gu, b_gu4,
                       moe_w_down, b_dn4, ln_g[1, 1], ln_b[1, 1])

    y_prompt = x_all[:SEQ][None]
    y_sample = x_all[SEQ:SEQ + N_SAMPLE][:, None]
    c_p_out = jnp.swapaxes(c_p, 1, 2)[None, None]
    n_p_out = n_p[:B_HEADS][None, None]
    m_p_out = m_p[:B_HEADS, 0][None, None]
    conv_p = conv_tail[5:8][None, None]
    return (y_prompt, y_sample, kv_p[0], kv_p[1], kv_p[2], c_p_out, n_p_out, m_p_out, conv_p,
            kv_s[0], kv_s[1], kv_s[2], c_s[None], n_s[None], m_s[:, :, 0][None], conv_s[None])
```

```python
import functools

import jax
import jax.numpy as jnp
from jax import lax
from jax.experimental import pallas as pl
from jax.experimental.pallas import tpu as pltpu

F32 = jnp.float32
BF16 = jnp.bfloat16
HIGHEST = lax.Precision.HIGHEST

D_MODEL = 1024
SEQ = 16384
N_SAMPLE = 32
PAST_LEN = 16384
DEPTH = 2
A_HEADS = 16
A_HEAD_DIM = 64
DILATIONS = (1, 4, 16)
WINDOWS = (128, 512, 2048)
ROPE_THETA = 10000.0
B_HEADS = 4
B_QK = 128
B_V = 256
N_EXPERTS = 32
TOP_K = 4
SWIGLU_LIMIT = 7.0
SWIGLU_ALPHA = 1.702
DN_ALPHA = (2 * DEPTH) ** 0.25
LN_EPS = 1e-5

LANES = 128
TM = 256
NT_P = SEQ // TM
NP = SEQ + TM
NT = NP // TM
ATT_TILE = 2048
QB = 128
N_ATT_TILES = SEQ // ATT_TILE
TAIL_T0 = NT_P - ATT_TILE // TM
EXP_TM = 256
N_ASSIGN = NP * TOP_K
N_EXP_TILES = (N_ASSIGN + N_EXPERTS * (EXP_TM - 1)) // EXP_TM + 1
N_SORTED = N_EXP_TILES * EXP_TM
VMEM_LIMIT = 56 * 1024 * 1024


def _cparams(sem):
    return pltpu.CompilerParams(dimension_semantics=sem, vmem_limit_bytes=VMEM_LIMIT)


def _layer_norm(z, g, b):
    mu = jnp.mean(z, axis=-1, keepdims=True)
    zc = z - mu
    var = jnp.mean(zc * zc, axis=-1, keepdims=True)
    return zc * lax.rsqrt(var + LN_EPS) * g + b


def _rope(res, cos, sin, n_lane_tiles, paired=False):
    rows = res.shape[0]
    lane = lax.broadcasted_iota(jnp.int32, (rows, LANES), 1)
    first = (lane % A_HEAD_DIM) < (A_HEAD_DIM // 2)
    outs = []
    for c in range(n_lane_tiles):
        seg = res[:, c * LANES:(c + 1) * LANES]
        if paired:
            rot = pltpu.roll(seg, LANES // 2, 1)
        else:
            rot = jnp.where(first, pltpu.roll(seg, LANES - 32, 1), pltpu.roll(seg, 32, 1))
        outs.append(seg * cos + rot * sin)
    return outs


PA_TM = 512


def _class_major(ref, lead, d):
    parts = []
    for r in range(d):
        rows = slice(None) if d == 1 else pl.ds(r, PA_TM // d, stride=d)
        parts.append(ref[rows, :] if lead is None else ref[lead, rows, :])
    return parts[0] if d == 1 else jnp.concatenate(parts, axis=0)


def _proj_a_kernel(x_ref, w_ref, cos_ref, sin_ref, o0_ref, o1_ref, o2_ref):
    j = pl.program_id(0)
    rotated = j < 2
    scale = jnp.where(j == 0, A_HEAD_DIM ** -0.5, 1.0).astype(F32)
    per_group = D_MODEL // LANES
    for g, (d, o_ref) in enumerate(zip(DILATIONS, (o0_ref, o1_ref, o2_ref))):
        xg = jnp.concatenate([_class_major(x_ref, c, d) for c in range(per_group)], axis=1)
        res = jnp.dot(xg.astype(BF16), w_ref[:, g * D_MODEL:(g + 1) * D_MODEL],
                      preferred_element_type=F32)
        cos = jnp.where(rotated, _class_major(cos_ref, None, d) * scale, 1.0)
        sin = jnp.where(rotated, _class_major(sin_ref, None, d) * scale, 0.0)
        out = jnp.concatenate(_rope(res, cos, sin, per_group, paired=True), axis=1).astype(BF16)
        n = PA_TM // d
        for r in range(d):
            o_ref[0, 0, r] = out[r * n:(r + 1) * n]


def _proj_a(x_slab, w_bf, cos, sin):
    outs = []
    specs = []
    for d in DILATIONS:
        outs.append(jax.ShapeDtypeStruct((3, N_ATT_TILES, d, ATT_TILE // d, D_MODEL), BF16))
        specs.append(pl.BlockSpec((1, 1, d, PA_TM // d, D_MODEL),
                                  lambda j, m: (j, m // (ATT_TILE // PA_TM), 0, m % (ATT_TILE // PA_TM), 0)))
    return pl.pallas_call(
        _proj_a_kernel,
        out_shape=outs,
        grid=(3, SEQ // PA_TM),
        in_specs=[
            pl.BlockSpec((D_MODEL // LANES, PA_TM, LANES), lambda j, m: (0, m, 0)),
            pl.BlockSpec((D_MODEL, 3 * D_MODEL), lambda j, m: (0, j)),
            pl.BlockSpec((PA_TM, LANES), lambda j, m: (m, 0)),
            pl.BlockSpec((PA_TM, LANES), lambda j, m: (m, 0)),
        ],
        out_specs=specs,
        compiler_params=_cparams(("arbitrary", "arbitrary")),
        name="proj_a",
    )(x_slab, w_bf, cos, sin)


def _proj_a_tail_kernel(x_ref, w_ref, cos_ref, sin_ref, o_ref):
    j = pl.program_id(0)
    xb = x_ref[...].astype(BF16)
    res = jnp.dot(xb, w_ref[...], preferred_element_type=F32)
    n_tiles = 3 * D_MODEL // LANES

    @pl.when(j < 2)
    def _():
        scale = jnp.where(j == 0, A_HEAD_DIM ** -0.5, 1.0).astype(F32)
        segs = _rope(res, cos_ref[...], sin_ref[...], n_tiles)
        for c in range(n_tiles):
            o_ref[0, :, c * LANES:(c + 1) * LANES] = segs[c] * scale

    @pl.when(j == 2)
    def _():
        o_ref[0] = res


def _proj_a_tail(x_all, w_bf, cos, sin):
    n = NT - TAIL_T0
    return pl.pallas_call(
        _proj_a_tail_kernel,
        out_shape=jax.ShapeDtypeStruct((3, n * TM, 3 * D_MODEL), F32),
        grid=(3, n),
        in_specs=[
            pl.BlockSpec((TM, D_MODEL), lambda j, m: (m + TAIL_T0, 0)),
            pl.BlockSpec((D_MODEL, 3 * D_MODEL), lambda j, m: (0, j)),
            pl.BlockSpec((TM, LANES), lambda j, m: (m + TAIL_T0, 0)),
            pl.BlockSpec((TM, LANES), lambda j, m: (m + TAIL_T0, 0)),
        ],
        out_specs=pl.BlockSpec((1, TM, 3 * D_MODEL), lambda j, m: (j, m, 0)),
        compiler_params=_cparams(("arbitrary", "arbitrary")),
        name="proj_a_tail",
    )(x_all, w_bf, cos, sin)


def _attn_kernel(*refs):
    in_refs = refs[:15]
    o_ref, oscr, lscr = refs[15:]
    jt = pl.program_id(0)
    lane = lax.broadcasted_iota(jnp.int32, (QB, LANES), 1)
    lo = lane < A_HEAD_DIM
    lo_qk = (lane // 32) % 2 == 0
    row = lax.broadcasted_iota(jnp.int32, (QB, 2 * QB), 0)
    col = lax.broadcasted_iota(jnp.int32, (QB, 2 * QB), 1)
    mask_cur = (col >= QB) & (col - QB <= row)
    mask_prev = (col < QB) & (col >= row)
    neg = jnp.float32(-jnp.inf)
    bias_full = jnp.concatenate([jnp.where(mask_cur | mask_prev, 0.0, neg)] * 2, axis=0)
    bias_first = jnp.concatenate([jnp.where(mask_cur, 0.0, neg)] * 2, axis=0)
    n_blocks = ATT_TILE // QB
    zero = jnp.zeros((QB, LANES), BF16)

    for g, d in enumerate(DILATIONS):
        q_ref, kc_ref, kp_ref, vc_ref, vp_ref = in_refs[5 * g:5 * g + 5]
        per_class = n_blocks // d
        for b in range(n_blocks):
            r, c = divmod(b, per_class)
            rows = pl.ds(c * QB, QB)
            q = q_ref[0, 0, r, rows, :]
            kc = kc_ref[0, 0, r, rows, :]
            vc = vc_ref[0, 0, r, rows, :]
            if c == 0:
                prow = pl.ds((per_class - 1) * QB, QB)
                kp = kp_ref[0, 0, r, prow, :]
                vp = vp_ref[0, 0, r, prow, :]
                bias = jnp.where(jt > 0, bias_full, bias_first)
            else:
                prow = pl.ds((c - 1) * QB, QB)
                kp = kc_ref[0, 0, r, prow, :]
                vp = vc_ref[0, 0, r, prow, :]
                bias = bias_full
            lhs = jnp.concatenate([jnp.where(lo_qk, q, zero), jnp.where(lo_qk, zero, q)], axis=0)
            k2 = jnp.concatenate([kp, kc], axis=0)
            v2 = jnp.concatenate([vp, vc], axis=0)
            s = lax.dot_general(lhs, k2, (((1,), (1,)), ((), ())), preferred_element_type=F32)
            s = s + bias
            mx = jnp.max(s, axis=1, keepdims=True)
            p = jnp.exp(s - mx)
            den = jnp.sum(p, axis=1, keepdims=True)
            o2 = jnp.dot(p.astype(BF16), v2, preferred_element_type=F32) / den
            lse = mx + jnp.log(den)
            o = jnp.where(lo, o2[:QB], o2[QB:])
            l = jnp.where(lo, jnp.broadcast_to(lse[:QB], (QB, LANES)),
                          jnp.broadcast_to(lse[QB:], (QB, LANES)))
            if d == 1:
                dst = pl.ds(b * QB, QB)
            else:
                dst = pl.ds(c * QB * d + r, QB, stride=d)
            oscr[g, dst, :] = o
            lscr[g, dst, :] = l

    def merge(i, carry):
        rows = pl.ds(pl.multiple_of(i * QB, QB), QB)
        l0, l1, l2 = lscr[0, rows, :], lscr[1, rows, :], lscr[2, rows, :]
        mx = jnp.maximum(jnp.maximum(l0, l1), l2)
        w0, w1, w2 = jnp.exp(l0 - mx), jnp.exp(l1 - mx), jnp.exp(l2 - mx)
        acc = oscr[0, rows, :] * w0 + oscr[1, rows, :] * w1 + oscr[2, rows, :] * w2
        o_ref[rows, :] = (acc / (w0 + w1 + w2)).astype(o_ref.dtype)
        return carry

    lax.fori_loop(0, n_blocks, merge, 0)


def _attn_prompt(qkv_groups):
    ins, specs = [], []
    for g, d in enumerate(DILATIONS):
        arr = qkv_groups[g]
        blk = (1, 1, d, ATT_TILE // d, LANES)
        cur = lambda s: (lambda j, h: (s, j, 0, 0, h))
        prev = lambda s: (lambda j, h: (s, jnp.maximum(j - 1, 0), 0, 0, h))
        for imap in (cur(0), cur(1), prev(1), cur(2), prev(2)):
            ins.append(arr)
            specs.append(pl.BlockSpec(blk, imap))
    return pl.pallas_call(
        _attn_kernel,
        out_shape=jax.ShapeDtypeStruct((SEQ, D_MODEL), BF16),
        grid=(N_ATT_TILES, D_MODEL // LANES),
        in_specs=specs,
        out_specs=pl.BlockSpec((ATT_TILE, LANES), lambda j, h: (j, h)),
        scratch_shapes=[pltpu.VMEM((3, ATT_TILE, LANES), F32),
                        pltpu.VMEM((3, ATT_TILE, LANES), F32)],
        compiler_params=_cparams(("arbitrary", "arbitrary")),
        name="attn_prompt",
    )(*ins)


SAMPLE_HEADS = 8


def _attn_sample_kernel(qkv_ref, c0_ref, c1_ref, c2_ref, o_ref):
    eye = (lax.broadcasted_iota(jnp.int32, (A_HEAD_DIM, A_HEAD_DIM), 0)
           == lax.broadcasted_iota(jnp.int32, (A_HEAD_DIM, A_HEAD_DIM), 1))

    def to_col(row):
        return jnp.sum(jnp.where(eye, jnp.broadcast_to(row, eye.shape), 0.0), axis=1, keepdims=True)

    def to_row(col):
        return jnp.sum(jnp.where(eye, jnp.broadcast_to(col, eye.shape), 0.0), axis=0, keepdims=True)

    rows = []
    for h in range(SAMPLE_HEADS):
        outs, lses = [], []
        for g, (d, c_ref) in enumerate(zip(DILATIONS, (c0_ref, c1_ref, c2_ref))):
            q = to_col(qkv_ref[0, g, h:h + 1, :])
            kn = to_col(qkv_ref[1, g, h:h + 1, :])
            vn = to_col(qkv_ref[2, g, h:h + 1, :])
            kt = c_ref[0, h]
            vt = c_ref[1, h]
            pos = lax.broadcasted_iota(jnp.int32, (1, kt.shape[1]), 1)
            s = jnp.sum(kt * q, axis=0, keepdims=True)
            s = jnp.where((pos & (d - 1)) == 0, s, -jnp.inf)
            s_self = jnp.sum(q * kn, axis=0, keepdims=True)
            mx = jnp.maximum(jnp.max(s, axis=1, keepdims=True), s_self)
            p = jnp.exp(s - mx)
            p_self = jnp.exp(s_self - mx)
            den = jnp.sum(p, axis=1, keepdims=True) + p_self
            acc = jnp.sum(vt * p, axis=1, keepdims=True) + p_self * vn
            outs.append(acc / den)
            lses.append(mx + jnp.log(den))
        mx = jnp.maximum(jnp.maximum(lses[0], lses[1]), lses[2])
        ws = [jnp.exp(l - mx) for l in lses]
        acc = outs[0] * ws[0] + outs[1] * ws[1] + outs[2] * ws[2]
        rows.append(to_row(acc / (ws[0] + ws[1] + ws[2])))
    o_ref[...] = jnp.concatenate(rows, axis=0)


def _attn_sample(qkv_s, caches_t):
    hg = SAMPLE_HEADS
    specs = [pl.BlockSpec((None, 3, 3, hg, A_HEAD_DIM), lambda b, h: (b, 0, 0, h, 0))]
    for c in caches_t:
        specs.append(pl.BlockSpec((None, 2, hg, A_HEAD_DIM, c.shape[-1]),
                                  lambda b, h: (b, 0, h, 0, 0)))
    return pl.pallas_call(
        _attn_sample_kernel,
        out_shape=jax.ShapeDtypeStruct((N_SAMPLE, A_HEADS, A_HEAD_DIM), F32),
        grid=(N_SAMPLE, A_HEADS // hg),
        in_specs=specs,
        out_specs=pl.BlockSpec((None, hg, A_HEAD_DIM), lambda b, h: (b, h, 0)),
        compiler_params=_cparams(("arbitrary", "arbitrary")),
        name="attn_sample",
    )(qkv_s, *caches_t)


def _out_proj_ln_kernel(yp_ref, ys_ref, x_ref, w_ref, g_ref, b_ref, o_ref):
    m = pl.program_id(0)
    y = jnp.where(m < NT_P, yp_ref[...], ys_ref[...])
    proj = jnp.dot(y, w_ref[...], preferred_element_type=F32)
    o_ref[...] = _layer_norm(DN_ALPHA * x_ref[...] + proj, g_ref[...], b_ref[...])


def _out_proj_ln(y_p, y_s, x_all, w_bf, g, b):
    return pl.pallas_call(
        _out_proj_ln_kernel,
        out_shape=jax.ShapeDtypeStruct((NP, D_MODEL), F32),
        grid=(NT,),
        in_specs=[
            pl.BlockSpec((TM, D_MODEL), lambda m: (jnp.minimum(m, NT_P - 1), 0)),
            pl.BlockSpec((TM, D_MODEL), lambda m: (0, 0)),
            pl.BlockSpec((TM, D_MODEL), lambda m: (m, 0)),
            pl.BlockSpec((D_MODEL, D_MODEL), lambda m: (0, 0)),
            pl.BlockSpec((1, D_MODEL), lambda m: (0, 0)),
            pl.BlockSpec((1, D_MODEL), lambda m: (0, 0)),
        ],
        out_specs=pl.BlockSpec((TM, D_MODEL), lambda m: (m, 0)),
        compiler_params=_cparams(("arbitrary",)),
        name="out_proj_ln",
    )(y_p, y_s, x_all, w_bf, g, b)


def _router_kernel(x_ref, w_ref, b_ref, route_ref, gate_ref, cnt_ref, run_scr):
    m = pl.program_id(0)

    @pl.when(m == 0)
    def _():
        run_scr[...] = jnp.zeros_like(run_scr)

    logits = jnp.dot(x_ref[...], w_ref[...], precision=HIGHEST,
                     preferred_element_type=F32) + b_ref[...]
    lane = lax.broadcasted_iota(jnp.int32, (TM, LANES), 1)
    tri = (lax.broadcasted_iota(jnp.int32, (TM, TM), 1)
           < lax.broadcasted_iota(jnp.int32, (TM, TM), 0)).astype(BF16)
    running = run_scr[0:1, :]
    packed = jnp.zeros((TM, LANES), F32)
    vals = jnp.zeros((TM, LANES), F32)
    work = logits
    for k in range(TOP_K):
        best = jnp.max(work, axis=1, keepdims=True)
        idx = jnp.min(jnp.where(work == best, lane, LANES), axis=1, keepdims=True)
        sel = lane == idx
        work = jnp.where(sel, -jnp.inf, work)
        onehot = sel.astype(BF16)
        before = jnp.dot(tri, onehot, preferred_element_type=F32) + running
        rank = jnp.sum(jnp.where(sel, before, 0.0), axis=1, keepdims=True)
        running = running + jnp.sum(onehot.astype(F32), axis=0, keepdims=True)
        packed = jnp.where(lane == k, idx.astype(F32), packed)
        packed = jnp.where(lane == TOP_K + k, rank, packed)
        vals = jnp.where(lane == k, best, vals)
    top = lane < TOP_K
    e = jnp.where(top, jnp.exp(vals - vals[:, 0:1]), 0.0)
    gate_ref[...] = e / jnp.sum(e, axis=1, keepdims=True)
    route_ref[...] = packed.T[:2 * TOP_K, :].astype(jnp.int32)
    run_scr[...] = jnp.broadcast_to(running, run_scr.shape)
    cnt_ref[...] = jnp.broadcast_to(running, cnt_ref.shape)


def _router(x_all, w_pad, b_pad):
    out =[jax.ShapeDtypeStruct((2 * TOP_K, NP), jnp.int32),
           jax.ShapeDtypeStruct((NP, LANES), F32),
           jax.ShapeDtypeStruct((8, LANES), F32)]
    return pl.pallas_call(
        _router_kernel,
        out_shape=out,
        grid=(NT,),
        in_specs=[pl.BlockSpec((TM, D_MODEL), lambda m: (m, 0)),
                  pl.BlockSpec((D_MODEL, LANES), lambda m: (0, 0)),
                  pl.BlockSpec((1, LANES), lambda m: (0, 0))],
        out_specs=[pl.BlockSpec((2 * TOP_K, TM), lambda m: (0, m)),
                   pl.BlockSpec((TM, LANES), lambda m: (m, 0)),
                   pl.BlockSpec((8, LANES), lambda m: (0, 0))],
        scratch_shapes=[pltpu.VMEM((8, LANES), F32)],
        compiler_params=_cparams(("arbitrary",)),
        name="router",
    )(x_all, w_pad, b_pad)


def _positions_kernel(start_ref, route_ref, pos_ref):
    ids = route_ref[0:TOP_K, :]
    acc = route_ref[TOP_K:2 * TOP_K, :]
    for e in range(N_EXPERTS):
        acc = acc + jnp.where(ids == e, start_ref[e], 0)
    pos_ref[...] = acc


def _positions(start, route):
    grid_spec = pltpu.PrefetchScalarGridSpec(
        num_scalar_prefetch=1,
        grid=(1,),
        in_specs=[pl.BlockSpec((2 * TOP_K, NP), lambda i, s: (0, 0))],
        out_specs=pl.BlockSpec((TOP_K, NP), lambda i, s: (0, 0)),
    )
    return pl.pallas_call(
        _positions_kernel,
        out_shape=jax.ShapeDtypeStruct((TOP_K, NP), jnp.int32),
        grid_spec=grid_spec,
        compiler_params=_cparams(("arbitrary",)),
        name="moe_positions",
    )(start, route)


def _dispatch_kernel(ltile_ref, nv_ref, pos_ref, x_ref, xs_ref, zbuf, sem, zsem):
    m = pl.program_id(0)

    @pl.when(m == 0)
    def _():
        zbuf[...] = jnp.zeros_like(zbuf)

        def zero_copy(tile):
            row = pl.multiple_of(tile * EXP_TM, EXP_TM)
            return pltpu.make_async_copy(zbuf, xs_ref.at[pl.ds(row, EXP_TM), :], zsem)

        for e in range(N_EXPERTS):
            @pl.when(ltile_ref[e] >= 0)
            def _():
                zero_copy(jnp.maximum(ltile_ref[e], 0)).start()

        def start_tail(i, carry):
            zero_copy(i).start()
            return carry

        def wait_tail(i, carry):
            zero_copy(i).wait()
            return carry

        lax.fori_loop(nv_ref[0], N_EXP_TILES, start_tail, 0)
        for e in range(N_EXPERTS):
            @pl.when(ltile_ref[e] >= 0)
            def _():
                zero_copy(jnp.maximum(ltile_ref[e], 0)).wait()
        lax.fori_loop(nv_ref[0], N_EXP_TILES, wait_tail, 0)

    def issue(t, carry):
        for k in range(TOP_K):
            pltpu.make_async_copy(x_ref.at[pl.ds(t, 1), :],
                                  xs_ref.at[pl.ds(pos_ref[k, t], 1), :],
                                  sem).start(priority=k % 2)
        return carry

    lax.fori_loop(0, TM, issue, 0, unroll=8)
    for k in range(TOP_K):
        pltpu.make_async_copy(x_ref, xs_ref.at[pl.ds(0, TM), :], sem).wait()


def _dispatch(last_tile, n_valid, pos, x_all):
    grid_spec = pltpu.PrefetchScalarGridSpec(
        num_scalar_prefetch=2,
        grid=(NT,),
        in_specs=[pl.BlockSpec((TOP_K, TM), lambda m, *_: (0, m), memory_space=pltpu.SMEM),
                  pl.BlockSpec((TM, D_MODEL), lambda m, *_: (m, 0))],
        out_specs=pl.BlockSpec(memory_space=pl.ANY),
        scratch_shapes=[pltpu.VMEM((EXP_TM, D_MODEL), F32),
                        pltpu.SemaphoreType.DMA, pltpu.SemaphoreType.DMA],
    )
    return pl.pallas_call(
        _dispatch_kernel,
        out_shape=jax.ShapeDtypeStruct((N_SORTED, D_MODEL), F32),
        grid_spec=grid_spec,
        compiler_params=_cparams(("arbitrary",)),
        name="moe_dispatch",
    )(last_tile, n_valid, pos, x_all)


def _experts_kernel(te_ref, nv_ref, nxt_ref, x_ref, bgu_ref, bdn_ref, wgu_hbm, wdn_hbm, o_ref,
                    stage_gu, stage_dn, wgu_bf, wdn_bf, sem, *, layer):
    i = pl.program_id(0)
    expert = te_ref[i]
    fresh = (i == 0) | (expert != te_ref[jnp.maximum(i - 1, 0)])

    def fetch(e):
        return (pltpu.make_async_copy(wgu_hbm.at[layer, e], stage_gu, sem.at[0]),
                pltpu.make_async_copy(wdn_hbm.at[layer, e], stage_dn, sem.at[1]))

    @pl.when(i == 0)
    def _():
        for copy in fetch(expert):
            copy.start()

    @pl.when(fresh)
    def _():
        for copy in fetch(expert):
            copy.wait()
        wgu_bf[...] = stage_gu[...].astype(BF16)
        wdn_bf[...] = stage_dn[...].astype(BF16)
        nxt = nxt_ref[expert]

        @pl.when(nxt >= 0)
        def _():
            for copy in fetch(nxt):
                copy.start()

    @pl.when(i < nv_ref[0])
    def _():
        xb = x_ref[...].astype(BF16)
        h = jnp.dot(xb, wgu_bf[...], preferred_element_type=F32) + bgu_ref[0]
        hg = jnp.minimum(h[:, :D_MODEL], SWIGLU_LIMIT)
        hu = jnp.clip(h[:, D_MODEL:], -SWIGLU_LIMIT, SWIGLU_LIMIT)
        act = hg * jax.nn.sigmoid(SWIGLU_ALPHA * hg) * (hu + 1.0)
        o_ref[...] = jnp.dot(act.astype(BF16), wdn_bf[...],
                             preferred_element_type=F32) + bdn_ref[0]

    @pl.when(i >= nv_ref[0])
    def _():
        o_ref[...] = jnp.zeros_like(o_ref)


def _experts(layer, tile_expert, n_valid, next_expert, x_sorted, w_gu, b_gu, w_dn, b_dn):
    def row_map(i, te, nv, nx):
        return (jnp.maximum(jnp.minimum(i, nv[0] - 1), 0), 0)

    def expert_map(i, te, nv, nx):
        return (layer, te[i], 0, 0)

    grid_spec = pltpu.PrefetchScalarGridSpec(
        num_scalar_prefetch=3,
        grid=(N_EXP_TILES,),
        in_specs=[
            pl.BlockSpec((EXP_TM, D_MODEL), row_map),
            pl.BlockSpec((None, 1, 1, 2 * D_MODEL), expert_map),
            pl.BlockSpec((None, 1, 1, D_MODEL), expert_map),
            pl.BlockSpec(memory_space=pl.ANY),
            pl.BlockSpec(memory_space=pl.ANY),
        ],
        out_specs=pl.BlockSpec((EXP_TM, D_MODEL), lambda i, te, nv, nx: (i, 0)),
        scratch_shapes=[pltpu.VMEM((D_MODEL, 2 * D_MODEL), F32),
                        pltpu.VMEM((D_MODEL, D_MODEL), F32),
                        pltpu.VMEM((D_MODEL, 2 * D_MODEL), BF16),
                        pltpu.VMEM((D_MODEL, D_MODEL), BF16),
                        pltpu.SemaphoreType.DMA((2,))],
    )
    return pl.pallas_call(
        functools.partial(_experts_kernel, layer=layer),
        out_shape=jax.ShapeDtypeStruct((N_SORTED, D_MODEL), F32),
        grid_spec=grid_spec,
        compiler_params=_cparams(("arbitrary",)),
        name="moe_experts",
    )(tile_expert, n_valid, next_expert, x_sorted, b_gu, b_dn, w_gu, w_dn)


CMB_TM = 256
N_CMB = NP // CMB_TM


def _combine_ln_kernel(pos_ref, nxt_ref, gate_ref, x_ref, g_ref, b_ref, ys_ref, o_ref, buf, sem):
    m = pl.program_id(0)

    def fetch(idx_ref, slot):
        def issue(t, carry):
            for k in range(TOP_K):
                pltpu.make_async_copy(ys_ref.at[pl.ds(idx_ref[k, t], 1), :],
                                      buf.at[slot, k, pl.ds(t, 1), :],
                                      sem.at[slot]).start(priority=k % 2)
            return carry

        lax.fori_loop(0, CMB_TM, issue, 0, unroll=8)

    @pl.when(m == 0)
    def _():
        fetch(pos_ref, 0)

    @pl.when(m + 1 < N_CMB)
    def _():
        fetch(nxt_ref, (m + 1) % 2)

    slot = m % 2
    pltpu.make_async_copy(buf.at[slot], buf.at[slot], sem.at[slot]).wait()
    gate = gate_ref[...]
    acc = buf[slot, 0] * gate[:, 0:1]
    for k in range(1, TOP_K):
        acc = acc + buf[slot, k] * gate[:, k:k + 1]
    o_ref[...] = _layer_norm(DN_ALPHA * x_ref[...] + acc, g_ref[...], b_ref[...])


def _combine_ln(pos, gate, x_all, g, b, y_sorted):
    idx = lambda off: pl.BlockSpec((TOP_K, CMB_TM),
                                   lambda m: (0, jnp.minimum(m + off, N_CMB - 1)),
                                   memory_space=pltpu.SMEM)
    return pl.pallas_call(
        _combine_ln_kernel,
        out_shape=jax.ShapeDtypeStruct((NP, D_MODEL), F32),
        grid=(N_CMB,),
        in_specs=[idx(0), idx(1),
                  pl.BlockSpec((CMB_TM, LANES), lambda m: (m, 0)),
                  pl.BlockSpec((CMB_TM, D_MODEL), lambda m: (m, 0)),
                  pl.BlockSpec((1, D_MODEL), lambda m: (0, 0)),
                  pl.BlockSpec((1, D_MODEL), lambda m: (0, 0)),
                  pl.BlockSpec(memory_space=pl.ANY)],
        out_specs=pl.BlockSpec((CMB_TM, D_MODEL), lambda m: (m, 0)),
        scratch_shapes=[pltpu.VMEM((2, TOP_K, CMB_TM, D_MODEL), F32),
                        pltpu.SemaphoreType.DMA((2,))],
        compiler_params=_cparams(("arbitrary",)),
        name="moe_combine_ln",
    )(pos, pos, gate, x_all, g, b, y_sorted)


def _moe_layer(layer, x_all, w_r, b_r, w_gu, b_gu, w_dn, b_dn, g, b):
    w_pad = jnp.pad(w_r, ((0, 0), (0, LANES - N_EXPERTS)))
    b_pad = jnp.pad(b_r, (0, LANES - N_EXPERTS), constant_values=-jnp.inf)[None]
    route, gate, counts = _router(x_all, w_pad, b_pad)
    counts = counts[0, :N_EXPERTS].astype(jnp.int32)
    tiles = (counts + EXP_TM - 1) // EXP_TM
    tile_end = jnp.cumsum(tiles)
    start = ((tile_end - tiles) * EXP_TM).astype(jnp.int32)
    last_tile = jnp.where(tiles > 0, tile_end - 1, -1).astype(jnp.int32)
    n_valid = tile_end[-1:].astype(jnp.int32)
    tile_ids = jnp.arange(N_EXP_TILES, dtype=jnp.int32)
    tile_expert = jnp.minimum(
        jnp.sum((tile_ids[:, None] >= tile_end[None, :]).astype(jnp.int32), axis=1),
        N_EXPERTS - 1).astype(jnp.int32)
    tile_expert = jnp.where(tile_ids < n_valid[0], tile_expert,
                            jnp.take(tile_expert, jnp.maximum(n_valid[0] - 1, 0)))
    experts = jnp.arange(N_EXPERTS, dtype=jnp.int32)
    later = (tiles > 0)[None, :] & (experts[None, :] > experts[:, None])
    next_expert = jnp.min(jnp.where(later, experts[None, :], N_EXPERTS), axis=1)
    next_expert = jnp.where(next_expert == N_EXPERTS, -1, next_expert).astype(jnp.int32)
    pos = _positions(start, route)
    x_sorted = _dispatch(last_tile, n_valid, pos, x_all)
    y_sorted = _experts(layer, tile_expert, n_valid, next_expert, x_sorted, w_gu, b_gu, w_dn,
                        b_dn)
    return _combine_ln(pos, gate, x_all, g[None], b[None], y_sorted)


CONV_K = 4
HK = B_HEADS * B_QK
HV = B_HEADS * B_V


def _conv_silu(prev3, cur, cw, cb):
    rows = cur.shape[0]
    ext = jnp.concatenate([jnp.zeros((5, 2 * HK), F32), prev3, cur], axis=0)
    acc = cb + cw[CONV_K - 1:CONV_K] * cur
    for j in range(CONV_K - 1):
        acc = acc + cw[j:j + 1] * ext[5 + j:5 + j + rows]
    return acc * jax.nn.sigmoid(acc)


def _proj_b_kernel(x_ref, w_ref, wg_ref, gb_ref, cw_ref, cb_ref,
                   q_ref, k_ref, v_ref, op_ref, gt_ref, conv_ref, carry):
    m = pl.program_id(0)

    @pl.when(m == 0)
    def _():
        carry[...] = jnp.zeros_like(carry)

    xb = x_ref[...].astype(BF16)
    p = jnp.dot(xb, w_ref[...], preferred_element_type=F32)
    gates = jnp.dot(xb, wg_ref[...], preferred_element_type=F32) + gb_ref[...]
    lane = lax.broadcasted_iota(jnp.int32, gates.shape, 1)
    is_f = (lane >= B_HEADS) & (lane < 2 * B_HEADS)
    gt_ref[...] = jnp.where(is_f, jax.nn.log_sigmoid(gates), gates)
    qk_pre = p[:, :2 * HK]
    qk = _conv_silu(carry[5:8, :], qk_pre, cw_ref[...], cb_ref[...])
    q_ref[...] = qk[:, :HK].astype(BF16)
    k_ref[...] = (qk[:, HK:] * (B_QK ** -0.5)).astype(BF16)
    v_ref[...] = p[:, 2 * HK:2 * HK + HV].astype(BF16)
    op_ref[...] = p[:, 2 * HK + HV:]
    carry[...] = qk_pre[TM - 8:, :]
    conv_ref[...] = qk_pre[TM - 8:, :]


def _proj_b(x_all, w_bf, wg_bf, gb, cw, cb):
    row = lambda n: pl.BlockSpec((TM, n), lambda m: (m, 0))
    const = lambda s: pl.BlockSpec(s, lambda m: (0, 0))
    out = [jax.ShapeDtypeStruct((SEQ, HK), BF16), jax.ShapeDtypeStruct((SEQ, HK), BF16),
           jax.ShapeDtypeStruct((SEQ, HV), BF16), jax.ShapeDtypeStruct((SEQ, HV), F32),
           jax.ShapeDtypeStruct((SEQ, LANES), F32), jax.ShapeDtypeStruct((8, 2 * HK), F32)]
    return pl.pallas_call(
        _proj_b_kernel,
        out_shape=out,
        grid=(NT_P,),
        in_specs=[row(D_MODEL), const((D_MODEL, 3 * D_MODEL)), const((D_MODEL, LANES)),
                  const((1, LANES)), const((CONV_K, 2 * HK)), const((1, 2 * HK))],
        out_specs=[row(HK), row(HK), row(HV), row(HV), row(LANES), const((8, 2 * HK))],
        scratch_shapes=[pltpu.VMEM((8, 2 * HK), F32)],
        compiler_params=_cparams(("arbitrary",)),
        name="proj_b",
    )(x_all, w_bf, wg_bf, gb, cw, cb)


CHUNK = 256


def _mlstm_kernel(q_ref, k_ref, v_ref, op_ref, gt_ref, nw_ref,
                  h_ref, c_out, n_out, m_out, c_scr, n_scr, m_scr):
    ci = pl.program_id(0)

    @pl.when(ci == 0)
    def _():
        c_scr[...] = jnp.zeros_like(c_scr)
        n_scr[...] = jnp.zeros_like(n_scr)
        m_scr[...] = jnp.zeros_like(m_scr)

    gates = gt_ref[...]
    t_idx = lax.broadcasted_iota(jnp.int32, (CHUNK, CHUNK), 0)
    s_idx = lax.broadcasted_iota(jnp.int32, (CHUNK, CHUNK), 1)
    causal = s_idx <= t_idx
    tri = causal.astype(F32)
    b_cols = jnp.dot(tri, gates, precision=HIGHEST, preferred_element_type=F32)
    gates_t = gates.T
    b_rows = jnp.dot(gates_t, (t_idx <= s_idx).astype(F32), precision=HIGHEST,
                     preferred_element_type=F32)
    for h in range(B_HEADS):
        q = q_ref[:, h * B_QK:(h + 1) * B_QK]
        k = k_ref[:, h * B_QK:(h + 1) * B_QK]
        v = v_ref[:, h * B_V:(h + 1) * B_V]
        m_prev = m_scr[h:h + 1, 0:1]
        n_prev = n_scr[h:h + 1, :]
        c_prev = c_scr[h]
        b_col = b_cols[:, B_HEADS + h:B_HEADS + h + 1]
        i_col = gates[:, h:h + 1]
        b_row = b_rows[B_HEADS + h:B_HEADS + h + 1, :]
        i_row = gates_t[h:h + 1, :]
        log_d = jnp.where(causal, b_col + (i_row - b_row), -jnp.inf)
        m_t = jnp.maximum(b_col + m_prev, jnp.max(log_d, axis=1, keepdims=True))
        dmat = jnp.exp(log_d - m_t)
        inter = jnp.exp(b_col + m_prev - m_t)
        s = lax.dot_general(q, k, (((1,), (1,)), ((), ())), preferred_element_type=F32) * dmat
        qf = q.astype(F32)
        num = (jnp.dot(s.astype(BF16), v, preferred_element_type=F32)
               + inter * jnp.dot(q, c_prev.astype(BF16), preferred_element_type=F32))
        den = (jnp.sum(s, axis=1, keepdims=True)
               + inter * jnp.sum(qf * n_prev, axis=1, keepdims=True))
        hv = num / jnp.maximum(jnp.abs(den), jnp.exp(-m_t))
        m_new = m_t[CHUNK - 1:CHUNK, :]
        b_last = b_col[CHUNK - 1:CHUNK, :]
        decay = jnp.exp(b_last - b_col + i_col - m_new)
        gfac = jnp.exp(b_last + m_prev - m_new)
        kd = k.astype(F32) * decay
        c_scr[h] = gfac * c_prev + jnp.dot(kd.T.astype(BF16), v, preferred_element_type=F32)
        n_scr[h:h + 1, :] = gfac * n_prev + jnp.sum(kd, axis=0, keepdims=True)
        m_scr[h:h + 1, :] = jnp.broadcast_to(m_new, (1, LANES))
        mu = jnp.mean(hv, axis=1, keepdims=True)
        hc = hv - mu
        var = jnp.mean(hc * hc, axis=1, keepdims=True)
        hn = hc * lax.rsqrt(var + LN_EPS)
        cols = slice(h * B_V, (h + 1) * B_V)
        hn = hn * nw_ref[:, cols] * jax.nn.sigmoid(op_ref[:, cols])
        h_ref[:, cols] = hn.astype(h_ref.dtype)
    c_out[...] = c_scr[...]
    n_out[...] = n_scr[...]
    m_out[...] = m_scr[...]


def _mlstm_prompt(q, k, v, o_pre, gates, norm_w):
    row = lambda n: pl.BlockSpec((CHUNK, n), lambda c: (c, 0))
    out = [jax.ShapeDtypeStruct((SEQ, HV), BF16),
           jax.ShapeDtypeStruct((B_HEADS, B_QK, B_V), F32),
           jax.ShapeDtypeStruct((8, LANES), F32),
           jax.ShapeDtypeStruct((8, LANES), F32)]
    return pl.pallas_call(
        _mlstm_kernel,
        out_shape=out,
        grid=(SEQ // CHUNK,),
        in_specs=[row(HK), row(HK), row(HV), row(HV), row(LANES),
                  pl.BlockSpec((1, HV), lambda c: (0, 0))],
        out_specs=[row(HV),
                   pl.BlockSpec((B_HEADS, B_QK, B_V), lambda c: (0, 0, 0)),
                   pl.BlockSpec((8, LANES), lambda c: (0, 0)),
                   pl.BlockSpec((8, LANES), lambda c: (0, 0))],
        scratch_shapes=[pltpu.VMEM((B_HEADS, B_QK, B_V), F32),
                        pltpu.VMEM((8, LANES), F32), pltpu.VMEM((8, LANES), F32)],
        compiler_params=_cparams(("arbitrary",)),
        name="mlstm_prompt",
    )(q, k, v, o_pre, gates, norm_w)


def _mlstm_sample_kernel(x_ref, w_ref, wg_ref, gb_ref, cw_ref, cb_ref, conv_ref,
                         c_ref, n_ref, m_ref, nw_ref,
                         h_ref, conv_out, c_out, n_out, m_out):
    xb = x_ref[...].astype(BF16)
    p = jnp.dot(xb, w_ref[...], preferred_element_type=F32)
    gates = jnp.dot(xb, wg_ref[...], preferred_element_type=F32) + gb_ref[...]
    qk_pre = p[0:1, :2 * HK]
    prev = conv_ref[0]
    cw = cw_ref[...]
    acc = cb_ref[...] + cw[3:4] * qk_pre
    for j in range(CONV_K - 1):
        acc = acc + cw[j:j + 1] * prev[j:j + 1]
    qk = acc * jax.nn.sigmoid(acc)
    conv_out[0] = jnp.concatenate([prev[1:3], qk_pre], axis=0)
    v_all = p[0:1, 2 * HK:2 * HK + HV]
    o_pre = p[0:1, 2 * HK + HV:]
    ones8 = jnp.ones((8, 1), F32)
    for h in range(B_HEADS):
        q = qk[:, h * B_QK:(h + 1) * B_QK]
        k = qk[:, HK + h * B_QK:HK + (h + 1) * B_QK] * (B_QK ** -0.5)
        v = v_all[:, h * B_V:(h + 1) * B_V]
        i_pre = gates[0:1, h:h + 1]
        log_f = jax.nn.log_sigmoid(gates[0:1, B_HEADS + h:B_HEADS + h + 1])
        m_prev = m_ref[0, h:h + 1, 0:1]
        n_prev = n_ref[0, h:h + 1, :]
        c_prev = c_ref[0, h]
        m_t = jnp.maximum(log_f + m_prev, i_pre)
        dgate = jnp.exp(i_pre - m_t)
        inter = jnp.exp(log_f + m_prev - m_t)
        s = jnp.sum(q * k, axis=1, keepdims=True) * dgate
        cq = lax.dot_general(ones8 * q, c_prev, (((1,), (1,)), ((), ())),
                             precision=HIGHEST, preferred_element_type=F32)[0:1]
        num = s * v + inter * cq
        den = s + inter * jnp.sum(n_prev * q, axis=1, keepdims=True)
        hv = num / jnp.maximum(jnp.abs(den), jnp.exp(-m_t))
        v_cols = jnp.broadcast_to(v, (B_QK, B_V)).T
        c_out[0, h] = inter * c_prev + dgate * (v_cols * k)
        n_out[0, h:h + 1, :] = inter * n_prev + dgate * k
        m_out[0, h:h + 1, :] = jnp.broadcast_to(m_t, (1, LANES))
        mu = jnp.mean(hv, axis=1, keepdims=True)
        hc = hv - mu
        var = jnp.mean(hc * hc, axis=1, keepdims=True)
        cols = slice(h * B_V, (h + 1) * B_V)
        hn = hc * lax.rsqrt(var + LN_EPS) * nw_ref[:, cols] * jax.nn.sigmoid(o_pre[:, cols])
        h_ref[0, :, cols] = jnp.broadcast_to(hn, (8, B_V))


def _mlstm_sample(x_s8, w_bf, wg_bf, gb, cw, cb, conv_state, c_state, n_state, m_state, norm_w):
    const = lambda s: pl.BlockSpec(s, lambda b: (0,) * len(s))
    per = lambda s: pl.BlockSpec((1,) + s, lambda b: (b,) + (0,) * len(s))
    out = [jax.ShapeDtypeStruct((N_SAMPLE, 8, HV), F32),
           jax.ShapeDtypeStruct((N_SAMPLE, 3, 2 * HK), F32),
           jax.ShapeDtypeStruct((N_SAMPLE, B_HEADS, B_V, B_QK), F32),
           jax.ShapeDtypeStruct((N_SAMPLE, B_HEADS, B_QK), F32),
           jax.ShapeDtypeStruct((N_SAMPLE, B_HEADS, LANES), F32)]
    return pl.pallas_call(
        _mlstm_sample_kernel,
        out_shape=out,
        grid=(N_SAMPLE,),
        in_specs=[pl.BlockSpec((8, D_MODEL), lambda b: (b, 0)),
                  const((D_MODEL, 3 * D_MODEL)), const((D_MODEL, LANES)), const((1, LANES)),
                  const((CONV_K, 2 * HK)), const((1, 2 * HK)),
                  per((3, 2 * HK)), per((B_HEADS, B_V, B_QK)), per((B_HEADS, B_QK)),
                  per((B_HEADS, LANES)), const((1, HV))],
        out_specs=[per((8, HV)), per((3, 2 * HK)), per((B_HEADS, B_V, B_QK)),
                   per((B_HEADS, B_QK)), per((B_HEADS, LANES))],
        compiler_params=_cparams(("arbitrary",)),
        name="mlstm_sample",
    )(x_s8, w_bf, wg_bf, gb, cw, cb, conv_state, c_state, n_state, m_state, norm_w)


def _rope_tables():
    half = A_HEAD_DIM // 2
    pos = jnp.concatenate([jnp.arange(SEQ, dtype=jnp.int32),
                           jnp.full((TM,), PAST_LEN, jnp.int32)])
    inv = ROPE_THETA ** (-jnp.arange(half, dtype=F32) / half)
    ang = pos.astype(F32)[:, None] * inv[None, :]
    cos, sin = jnp.cos(ang), jnp.sin(ang)
    cos_t = jnp.tile(cos, (1, LANES // half))
    sin_t = jnp.tile(jnp.concatenate([-sin, sin], axis=1), (1, LANES // A_HEAD_DIM))
    sin_paired = jnp.concatenate([-sin, -sin, sin, sin], axis=1)
    return cos_t, sin_t, sin_paired


def _paired_qk_columns(w_bf):
    half = A_HEAD_DIM // 2
    qk = w_bf[:, :2 * 3 * D_MODEL].reshape(D_MODEL, -1, 2, 2, half)
    qk = jnp.swapaxes(qk, 2, 3).reshape(D_MODEL, 2 * 3 * D_MODEL)
    return jnp.concatenate([qk, w_bf[:, 2 * 3 * D_MODEL:]], axis=1)


def kernel(x_prompt, x_sample, cache_kv_w128, cache_kv_w512, cache_kv_w2048, state_C, state_n,
           state_m, state_conv, a_w_in, a_w_out, b_w_in, b_conv_w, b_conv_b, b_gate_b, b_norm_w,
           b_w_out, ln_g, ln_b, moe_w_router, moe_b_router, moe_w_gu, moe_b_gu, moe_w_down,
           moe_b_down):
    pad = jnp.zeros((TM - N_SAMPLE, D_MODEL), F32)
    x_all = jnp.concatenate([x_prompt[0], x_sample[:, 0], pad], axis=0)

    cos_t, sin_t, sin_paired = _rope_tables()
    w_in = a_w_in[0].astype(BF16)
    x_slab = jnp.transpose(x_all.reshape(NP, D_MODEL // LANES, LANES), (1, 0, 2))
    groups = _proj_a(x_slab, _paired_qk_columns(w_in), cos_t, sin_paired)
    tail = _proj_a_tail(x_all, w_in, cos_t, sin_t)
    o_p = _attn_prompt(groups)
    n_tail = ATT_TILE
    qkv_s = tail[:, n_tail:n_tail + N_SAMPLE].reshape(3, N_SAMPLE, 3, A_HEADS, A_HEAD_DIM)
    qkv_s = jnp.transpose(qkv_s, (1, 0, 2, 3, 4))
    caches_t = [jnp.transpose(c[0], (0, 2, 3, 4, 1))
                for c in (cache_kv_w128, cache_kv_w512, cache_kv_w2048)]
    o_s = _attn_sample(qkv_s, caches_t).reshape(N_SAMPLE, D_MODEL)
    o_s = jnp.concatenate([o_s, pad], axis=0).astype(BF16)
    x_all = _out_proj_ln(o_p, o_s, x_all, a_w_out[0].astype(BF16), ln_g[0, 0][None], ln_b[0, 0][None])
    b_gu4, b_dn4 = moe_b_gu[:, :, None, :], moe_b_down[:, :, None, :]
    x_all = _moe_layer(0, x_all, moe_w_router[0], moe_b_router[0], moe_w_gu, b_gu4,
                       moe_w_down, b_dn4, ln_g[0, 1], ln_b[0, 1])

    kv_p, kv_s = [], []
    for g, (d, w) in enumerate(zip(DILATIONS, WINDOWS)):
        k_nat = tail[1, n_tail - w:n_tail, g * D_MODEL:(g + 1) * D_MODEL]
        v_nat = tail[2, n_tail - w:n_tail, g * D_MODEL:(g + 1) * D_MODEL]
        kv = jnp.stack([k_nat, v_nat], axis=1).reshape(1, 1, w, 2, A_HEADS, A_HEAD_DIM)
        kv_p.append(kv)
        kv_s.append(jnp.stack([qkv_s[:, 1, g], qkv_s[:, 2, g]], axis=1)[None, :, None])

    w_b = b_w_in[0]
    w_main = w_b[:, :3 * D_MODEL].astype(BF16)
    w_gate = jnp.pad(w_b[:, 3 * D_MODEL:], ((0, 0), (0, LANES - 2 * B_HEADS))).astype(BF16)
    gb = jnp.pad(b_gate_b[0].reshape(-1), (0, LANES - 2 * B_HEADS))[None]
    cw, cb, nw = b_conv_w[0], b_conv_b[0][None], b_norm_w[0][None]
    q, k, v, o_pre, gates, conv_tail = _proj_b(x_all, w_main, w_gate, gb, cw, cb)
    h_p, c_p, n_p, m_p = _mlstm_prompt(q, k, v, o_pre, gates, nw)
    x_s = x_all[SEQ:SEQ + N_SAMPLE]
    x_s8 = jnp.pad(x_s[:, None, :], ((0, 0), (0, 7), (0, 0))).reshape(N_SAMPLE * 8, D_MODEL)
    m_in = jnp.broadcast_to(state_m[0][:, :, None], (N_SAMPLE, B_HEADS, LANES))
    h_s, conv_s, c_s, n_s, m_s = _mlstm_sample(x_s8, w_main, w_gate, gb, cw, cb, state_conv[0],
                                               state_C[0], state_n[0], m_in, nw)
    h_s = jnp.concatenate([h_s[:, 0, :], pad], axis=0).astype(BF16)
    x_all = _out_proj_ln(h_p, h_s, x_all, b_w_out[0].astype(BF16), ln_g[1, 0][None], ln_b[1, 0][None])
    x_all = _moe_layer(1, x_all, moe_w_router[1], moe_b_router[1], moe_w_gu, b_gu4,
                       moe_w_down, b_dn4, ln_g[1, 1], ln_b[1, 1])

    y_prompt = x_all[:SEQ][None]
    y_sample = x_all[SEQ:SEQ + N_SAMPLE][:, None]
    c_p_out = jnp.swapaxes(c_p, 1, 2)[None, None]
    n_p_out = n_p[:B_HEADS][None, None]
    m_p_out = m_p[:B_HEADS, 0][None, None]
    conv_p = conv_tail[5:8][None, None]
    return (y_prompt, y_sample, kv_p[0], kv_p[1], kv_p[2], c_p_out, n_p_out, m_p_out, conv_p,
            kv_s[0], kv_s[1], kv_s[2], c_s[None], n_s[None], m_s[:, :, 0][None], conv_s[None])
```
